```python
import math
import jax
import jax.numpy as jnp
from jax import lax
import numpy as np

D_MODEL = 1024
BATCH = 16
SEQ = 4096
DEPTH = 4
DEC_BATCH = 32
DEC_SEQ = 2048
PAST_LEN = 128

MLA_HEADS = 8
Q_LORA_RANK = 256
KV_LORA_RANK = 128
QK_NOPE_DIM = 64
QK_ROPE_DIM = 32
V_HEAD_DIM = 64
ROPE_THETA = 10000.0
MLA_Q_BLOCK = 128

DIL_PATTERNS = ((128, 1), (512, 4), (2048, 16))
DIL_HEADS_PER_GROUP = 4
DIL_HEADS = DIL_HEADS_PER_GROUP * len(DIL_PATTERNS)
DIL_HEAD_DIM = 128

D_FF = 2816
RMS_EPS = 1e-6
NEG_INF = -1e30
N_BRANCHES = 2

MLA_OUT = MLA_HEADS * V_HEAD_DIM
DIL_OUT = DIL_HEADS_PER_GROUP * DIL_HEAD_DIM
MLA_IN = Q_LORA_RANK + KV_LORA_RANK + QK_ROPE_DIM
DIL_QKV = DIL_HEADS * DIL_HEAD_DIM
SPLIT_POINTS = (Q_LORA_RANK, Q_LORA_RANK + KV_LORA_RANK, MLA_IN, MLA_IN + DIL_QKV, MLA_IN + 2 * DIL_QKV, MLA_IN + 3 * DIL_QKV)
W_IN_COLS = MLA_IN + 3 * DIL_QKV + N_BRANCHES * D_MODEL

kernel_name = 'hybrid_mla_dilated_encoder'


def _rmsnorm(x, g):
    xf = x.astype(jnp.float32)
    xf = xf * lax.rsqrt(jnp.mean(xf * xf, axis=-1, keepdims=True) + RMS_EPS)
    return (xf * g.astype(jnp.float32)).astype(x.dtype)


def _swiglu(x, w_gate, w_up, w_down):
    return (jax.nn.silu(x @ w_gate) * (x @ w_up)) @ w_down


def _rope_tables(seq):
    pos = jnp.arange(seq, dtype=jnp.float32)
    inv_freq = 1.0 / (ROPE_THETA ** (jnp.arange(0, QK_ROPE_DIM, 2, dtype=jnp.float32) / QK_ROPE_DIM))
    ang = pos[:, None] * inv_freq[None, :]
    return jnp.cos(ang), jnp.sin(ang)


def _apply_rope(x, cos, sin):
    half = x.shape[-1] // 2
    xf = x.astype(jnp.float32)
    x1, x2 = xf[..., :half], xf[..., half:]
    return jnp.concatenate([x1 * cos - x2 * sin, x2 * cos + x1 * sin], axis=-1).astype(x.dtype)


def _alibi_slopes(n):
    return 2.0 ** (-8.0 * jnp.arange(1, n + 1, dtype=jnp.float32) / n)


def _mla_attention(c_q, c_kv, k_pe, q_a_norm, w_q_up, kv_a_norm, w_kv_up):
    b, s, _ = c_q.shape
    q = (_rmsnorm(c_q, q_a_norm) @ w_q_up).reshape(b, s, MLA_HEADS, QK_NOPE_DIM + QK_ROPE_DIM)
    kv = (_rmsnorm(c_kv, kv_a_norm) @ w_kv_up).reshape(b, s, MLA_HEADS, QK_NOPE_DIM + V_HEAD_DIM)
    q_nope, q_pe = q[..., :QK_NOPE_DIM], q[..., QK_NOPE_DIM:]
    k_nope, v = kv[..., :QK_NOPE_DIM], kv[..., QK_NOPE_DIM:]
    cos, sin = _rope_tables(s)
    q_pe = _apply_rope(q_pe, cos[:, None, :], sin[:, None, :])
    k_pe = _apply_rope(k_pe, cos, sin)
    scale = (QK_NOPE_DIM + QK_ROPE_DIM) ** -0.5
    n_blk = s // MLA_Q_BLOCK

    def to_blocks(t):
        return t.reshape(b, n_blk, MLA_Q_BLOCK, MLA_HEADS, t.shape[-1]).transpose(1, 0, 2, 3, 4)

    def attend_block(qs):
        qn, qr = qs
        logits = (jnp.einsum('bqhd,bkhd->bhqk', qn, k_nope, preferred_element_type=jnp.float32)
                  + jnp.einsum('bqhd,bkd->bhqk', qr, k_pe, preferred_element_type=jnp.float32))
        p = jax.nn.softmax(logits * scale, axis=-1).astype(v.dtype)
        return jnp.einsum('bhqk,bkhd->bqhd', p, v)

    o = lax.map(attend_block, (to_blocks(q_nope), to_blocks(q_pe)))
    return o.transpose(1, 0, 2, 3, 4).reshape(b, s, MLA_OUT)


def _dilated_group_attention(q, k, v, window, dilation, slopes):
    b, s, hg, dh = q.shape
    half = window // (2 * dilation)
    seg = s // dilation
    n_blk = -(-seg // half)
    seg_p = n_blk * half

    def strided(t):
        return t.reshape(b, seg, dilation, hg, dh).transpose(0, 2, 1, 3, 4)

    qs = jnp.pad(strided(q), ((0, 0), (0, 0), (0, seg_p - seg), (0, 0), (0, 0))).reshape(b, dilation, n_blk, half, hg, dh)
    kv_pad = ((0, 0), (0, 0), (half, seg_p - seg + half), (0, 0), (0, 0))
    kp = jnp.pad(strided(k), kv_pad)
    vp = jnp.pad(strided(v), kv_pad)

    def key_blocks(t):
        return jnp.concatenate([t[:, :, o * half:o * half + seg_p].reshape(b, dilation, n_blk, half, hg, dh) for o in range(3)], axis=3)

    kb, vb = key_blocks(kp), key_blocks(vp)
    logits = jnp.einsum('brnqhd,brnkhd->brnhqk', qs, kb, preferred_element_type=jnp.float32) * (dh ** -0.5)
    rel = (jnp.arange(3 * half)[None, :] - half) - jnp.arange(half)[:, None]
    kpos = jnp.arange(n_blk)[:, None] * half + jnp.arange(3 * half)[None, :] - half
    allowed = (jnp.abs(rel) <= half)[None] & ((kpos >= 0) & (kpos < seg))[:, None, :]
    alibi = -slopes[:, None, None] * (dilation * jnp.abs(rel)).astype(jnp.float32)[None]
    logits = jnp.where(allowed[None, None, :, None], logits + alibi[None, None, None], NEG_INF)
    m = jnp.max(logits, axis=-1)
    p = jnp.exp(logits - m[..., None])
    l = jnp.sum(p, axis=-1)
    o = jnp.einsum('brnhqk,brnkhd->brnqhd', p.astype(v.dtype), vb)
    o = (o.astype(jnp.float32) / jnp.swapaxes(l, -1, -2)[..., None]).astype(v.dtype)
    lse = jnp.swapaxes(m + jnp.log(l), -1, -2)
    o = o.reshape(b, dilation, seg_p, hg, dh)[:, :, :seg].transpose(0, 2, 1, 3, 4).reshape(b, s, hg, dh)
    lse = lse.reshape(b, dilation, seg_p, hg)[:, :, :seg].transpose(0, 2, 1, 3).reshape(b, s, hg)
    return o, lse


def _hybrid_mixer(h, w_in, b_gate, q_a_norm, w_q_up, kv_a_norm, w_kv_up, w_branch_mla, w_branch_dil, w_out):
    b, s, _ = h.shape
    z = h @ w_in
    c_q, c_kv, k_pe, dq, dk, dv, gate_logits = jnp.split(z, SPLIT_POINTS, axis=-1)
    o_mla = _mla_attention(c_q, c_kv, k_pe, q_a_norm, w_q_up, kv_a_norm, w_kv_up)
    dq = dq.reshape(b, s, DIL_HEADS, DIL_HEAD_DIM)
    dk = dk.reshape(b, s, DIL_HEADS, DIL_HEAD_DIM)
    dv = dv.reshape(b, s, DIL_HEADS, DIL_HEAD_DIM)
    slopes = _alibi_slopes(DIL_HEADS)
    outs, lses = [], []
    for g, (window, dilation) in enumerate(DIL_PATTERNS):
        hs = slice(g * DIL_HEADS_PER_GROUP, (g + 1) * DIL_HEADS_PER_GROUP)
        o_g, lse_g = _dilated_group_attention(dq[:, :, hs], dk[:, :, hs], dv[:, :, hs], window, dilation, slopes[hs])
        outs.append(o_g)
        lses.append(lse_g)
    w = jax.nn.softmax(jnp.stack(lses, axis=0), axis=0)
    o_dil = jnp.sum(w[..., None] * jnp.stack(outs, axis=0).astype(jnp.float32), axis=0).astype(h.dtype).reshape(b, s, DIL_OUT)
    gates = jax.nn.sigmoid(gate_logits + b_gate)
    g_mla, g_dil = gates[..., :D_MODEL], gates[..., D_MODEL:]
    merged = g_mla * (o_mla @ w_branch_mla) + g_dil * (o_dil @ w_branch_dil)
    return merged @ w_out


def _trunk(x, layer_params, final_norm):
    (ffn1_norm, ffn1_w_gate, ffn1_w_up, ffn1_w_down, mix_norm, w_in, b_gate, q_a_norm, w_q_up,
     kv_a_norm, w_kv_up, w_branch_mla, w_branch_dil, w_out, ffn2_norm, ffn2_w_gate, ffn2_w_up, ffn2_w_down) = layer_params
    for l in range(DEPTH):
        x = x + 0.5 * _swiglu(_rmsnorm(x, ffn1_norm[l]), ffn1_w_gate[l], ffn1_w_up[l], ffn1_w_down[l])
        x = x + _hybrid_mixer(_rmsnorm(x, mix_norm[l]), w_in[l], b_gate[l], q_a_norm[l], w_q_up[l], kv_a_norm[l],
                              w_kv_up[l], w_branch_mla[l], w_branch_dil[l], w_out[l])
        x = x + 0.5 * _swiglu(_rmsnorm(x, ffn2_norm[l]), ffn2_w_gate[l], ffn2_w_up[l], ffn2_w_down[l])
    return _rmsnorm(x, final_norm)


def setup_inputs(seed: int = 0) -> dict:
    key = jax.random.key(seed)
    ks = jax.random.split(key, 22)

    def dense(k, shape):
        return jax.random.normal(k, shape, jnp.float32) * (shape[-2] ** -0.5)

    def gain(k, shape):
        return 1.0 + 0.02 * jax.random.normal(k, shape, jnp.float32)

    return {
        'x_prompt': jax.random.normal(ks[0], (BATCH, SEQ, D_MODEL), jnp.float32),
        'x_sample': jax.random.normal(ks[1], (DEC_BATCH, DEC_SEQ, D_MODEL), jnp.float32),
        'ffn1_norm': gain(ks[2], (DEPTH, D_MODEL)),
        'ffn1_w_gate': dense(ks[3], (DEPTH, D_MODEL, D_FF)),
        'ffn1_w_up': dense(ks[4], (DEPTH, D_MODEL, D_FF)),
        'ffn1_w_down': dense(ks[5], (DEPTH, D_FF, D_MODEL)),
        'mix_norm': gain(ks[6], (DEPTH, D_MODEL)),
        'w_in': dense(ks[7], (DEPTH, D_MODEL, W_IN_COLS)),
        'b_gate': 0.01 * jax.random.normal(ks[8], (DEPTH, N_BRANCHES * D_MODEL), jnp.float32),
        'q_a_norm': gain(ks[9], (DEPTH, Q_LORA_RANK)),
        'w_q_up': dense(ks[10], (DEPTH, Q_LORA_RANK, MLA_HEADS * (QK_NOPE_DIM + QK_ROPE_DIM))),
        'kv_a_norm': gain(ks[11], (DEPTH, KV_LORA_RANK)),
        'w_kv_up': dense(ks[12], (DEPTH, KV_LORA_RANK, MLA_HEADS * (QK_NOPE_DIM + V_HEAD_DIM))),
        'w_branch_mla': dense(ks[13], (DEPTH, MLA_OUT, D_MODEL)),
        'w_branch_dil': dense(ks[14], (DEPTH, DIL_OUT, D_MODEL)),
        'w_out': dense(ks[15], (DEPTH, D_MODEL, D_MODEL)),
        'ffn2_norm': gain(ks[16], (DEPTH, D_MODEL)),
        'ffn2_w_gate': dense(ks[17], (DEPTH, D_MODEL, D_FF)),
        'ffn2_w_up': dense(ks[18], (DEPTH, D_MODEL, D_FF)),
        'ffn2_w_down': dense(ks[19], (DEPTH, D_FF, D_MODEL)),
        'final_norm': gain(ks[20], (D_MODEL,)),
    }


def reference(x_prompt, x_sample, ffn1_norm, ffn1_w_gate, ffn1_w_up, ffn1_w_down, mix_norm, w_in, b_gate,
              q_a_norm, w_q_up, kv_a_norm, w_kv_up, w_branch_mla, w_branch_dil, w_out,
              ffn2_norm, ffn2_w_gate, ffn2_w_up, ffn2_w_down, final_norm):
    layer_params = (ffn1_norm, ffn1_w_gate, ffn1_w_up, ffn1_w_down, mix_norm, w_in, b_gate, q_a_norm, w_q_up,
                    kv_a_norm, w_kv_up, w_branch_mla, w_branch_dil, w_out, ffn2_norm, ffn2_w_gate, ffn2_w_up, ffn2_w_down)
    y_prompt = _trunk(x_prompt, layer_params, final_norm)
    y_sample = _trunk(x_sample, layer_params, final_norm)
    return (y_prompt, y_sample)
```

```python
import functools
import math

import jax
import jax.numpy as jnp
from jax import lax
from jax.experimental import pallas as pl
from jax.experimental.pallas import tpu as pltpu

D_MODEL = 1024
DEPTH = 4

MLA_HEADS = 8
Q_LORA_RANK = 256
KV_LORA_RANK = 128
QK_NOPE_DIM = 64
QK_ROPE_DIM = 32
V_HEAD_DIM = 64
ROPE_THETA = 10000.0
MLA_HEAD_PAD = 128
MLA_OUT = MLA_HEADS * V_HEAD_DIM

DIL_PATTERNS = ((128, 1), (512, 4), (2048, 16))
DIL_HEADS_PER_GROUP = 4
DIL_HEADS = DIL_HEADS_PER_GROUP * len(DIL_PATTERNS)
DIL_HEAD_DIM = 128
DIL_OUT = DIL_HEADS_PER_GROUP * DIL_HEAD_DIM
DIL_QKV = DIL_HEADS * DIL_HEAD_DIM

D_FF = 2816
RMS_EPS = 1e-6
NEG_INF = -1e30

MLA_IN = Q_LORA_RANK + KV_LORA_RANK + QK_ROPE_DIM

V7X_VMEM_LIMIT_BYTES = 56 * 1024 * 1024

TOKEN_TILE = 512
MLA_Q_TILE = 512
DIL_Q_BLOCK = 128
DIL_MAX_BLOCK_ROWS = 4096

_BF16 = jnp.bfloat16
_F32 = jnp.float32
_NT = (((1,), (1,)), ((), ()))
_TN = (((0,), (0,)), ((), ()))


def _params(*semantics):
    return pltpu.CompilerParams(dimension_semantics=semantics, vmem_limit_bytes=V7X_VMEM_LIMIT_BYTES)


def _resident(shape):
    return pl.BlockSpec(shape, lambda *_: (0,) * len(shape), pipeline_mode=pl.Buffered(1))


def _rms(xf, g):
    return xf * lax.rsqrt(jnp.mean(xf * xf, axis=-1, keepdims=True) + RMS_EPS) * g


def _dot(a, b):
    return jnp.dot(a, b, preferred_element_type=_F32)


def _ffn_kernel(x_ref, g_ref, wg_ref, wu_ref, wd_ref, fg_ref, o_ref, *, final_norm):
    x = x_ref[...]
    h = _rms(x, g_ref[...]).astype(_BF16)
    gate = _dot(h, wg_ref[...])
    up = _dot(h, wu_ref[...])
    act = (gate * jax.nn.sigmoid(gate) * up).astype(_BF16)
    y = x + 0.5 * _dot(act, wd_ref[...])
    if final_norm:
        y = _rms(y, fg_ref[...])
    o_ref[...] = y


def _ffn(x, norm_g, wg, wu, wd, final_g, *, final_norm):
    n_tok = x.shape[0]
    tm = TOKEN_TILE
    row = pl.BlockSpec((tm, D_MODEL), lambda i: (i, 0))
    return pl.pallas_call(
        functools.partial(_ffn_kernel, final_norm=final_norm),
        grid=(n_tok // tm,),
        in_specs=[row, _resident((1, D_MODEL)), _resident((D_MODEL, D_FF)), _resident((D_MODEL, D_FF)),
                  _resident((D_FF, D_MODEL)), _resident((1, D_MODEL))],
        out_specs=row,
        out_shape=jax.ShapeDtypeStruct((n_tok, D_MODEL), _F32),
        compiler_params=_params("parallel"),
        name="ffn",
    )(x, norm_g, wg, wu, wd, final_g)


def _proj_kernel(x_ref, g_ref, cos_ref, sin_ref, w_lat_ref, w_dil_ref, w_gate_ref, b_gate_ref, qn_ref, kvn_ref,
                 wq_ref, wq_rot_ref, wk_ref, wvt_ref,
                 q_ref, k_ref, vt_ref, dq_ref, dk_ref, dv_ref, gates_ref):
    h = _rms(x_ref[...], g_ref[...]).astype(_BF16)
    cos = cos_ref[...]
    sin = sin_ref[...]

    for j, ref in enumerate((dq_ref, dk_ref, dv_ref)):
        ref[...] = _dot(h, w_dil_ref[:, j * DIL_QKV:(j + 1) * DIL_QKV]).astype(_BF16)
    gates_ref[...] = jax.nn.sigmoid(_dot(h, w_gate_ref[...]) + b_gate_ref[...]).astype(_BF16)

    lat = _dot(h, w_lat_ref[...])
    c_q = _rms(lat[:, :Q_LORA_RANK], qn_ref[...]).astype(_BF16)
    c_kv = _rms(lat[:, Q_LORA_RANK:Q_LORA_RANK + KV_LORA_RANK], kvn_ref[...]).astype(_BF16)
    o = Q_LORA_RANK + KV_LORA_RANK
    k_pe = lat[:, o:o + MLA_HEAD_PAD] * cos + lat[:, o + MLA_HEAD_PAD:o + 2 * MLA_HEAD_PAD] * sin

    q = _dot(c_q, wq_ref[...])
    q_rot = _dot(c_q, wq_rot_ref[...])
    k_nope = _dot(c_kv, wk_ref[...])
    for hd in range(MLA_HEADS):
        hs = slice(hd * MLA_HEAD_PAD, (hd + 1) * MLA_HEAD_PAD)
        q_ref[:, hs] = (q[:, hs] * cos + q_rot[:, hs] * sin).astype(_BF16)
        k_ref[:, hs] = (k_nope[:, hs] + k_pe).astype(_BF16)
    vt_ref[0] = lax.dot_general(wvt_ref[...], c_kv, _NT, preferred_element_type=_F32).astype(_BF16)


def _proj(x, seq, lw, cos_t, sin_t):
    n_tok = x.shape[0]
    tm = TOKEN_TILE
    tiles_per_seq = seq // tm
    row = lambda w: pl.BlockSpec((tm, w), lambda i: (i, 0))
    rope = pl.BlockSpec((tm, MLA_HEAD_PAD), lambda i: (i % tiles_per_seq, 0))
    n_lat = lw["w_lat"].shape[1]
    mla_w = MLA_HEADS * MLA_HEAD_PAD
    out_shape = (
        jax.ShapeDtypeStruct((n_tok, mla_w), _BF16),
        jax.ShapeDtypeStruct((n_tok, mla_w), _BF16),
        jax.ShapeDtypeStruct((n_tok // tm, MLA_OUT, tm), _BF16),
        jax.ShapeDtypeStruct((n_tok, DIL_QKV), _BF16),
        jax.ShapeDtypeStruct((n_tok, DIL_QKV), _BF16),
        jax.ShapeDtypeStruct((n_tok, DIL_QKV), _BF16),
        jax.ShapeDtypeStruct((n_tok, 2 * D_MODEL), _BF16),
    )
    out_specs = (row(mla_w), row(mla_w), pl.BlockSpec((1, MLA_OUT, tm), lambda i: (i, 0, 0)),
                 row(DIL_QKV), row(DIL_QKV), row(DIL_QKV), row(2 * D_MODEL))
    return pl.pallas_call(
        _proj_kernel,
        grid=(n_tok // tm,),
        in_specs=[row(D_MODEL), _resident((1, D_MODEL)), rope, rope,
                  _resident((D_MODEL, n_lat)), _resident((D_MODEL, 3 * DIL_QKV)), _resident((D_MODEL, 2 * D_MODEL)),
                  _resident((1, 2 * D_MODEL)), _resident((1, Q_LORA_RANK)), _resident((1, KV_LORA_RANK)),
                  _resident((Q_LORA_RANK, mla_w)), _resident((Q_LORA_RANK, mla_w)),
                  _resident((KV_LORA_RANK, mla_w)), _resident((MLA_OUT, KV_LORA_RANK))],
        out_specs=out_specs,
        out_shape=out_shape,
        compiler_params=_params("parallel"),
        name="proj",
    )(x, lw["mix_norm"], cos_t, sin_t, lw["w_lat"], lw["w_dil"], lw["w_gate"], lw["b_gate"], lw["q_a_norm"],
      lw["kv_a_norm"], lw["wq"], lw["wq_rot"], lw["wk"], lw["wvt"])


def _mla_kernel(q_ref, k_ref, vt_ref, o_ref, *, n_chunks, chunk, exp2_scale):
    q = q_ref[...]
    tq = q.shape[0]

    def body(c, carry):
        m, l, acc = carry
        ks = pl.multiple_of(c * chunk, chunk)
        s_t = lax.dot_general(k_ref[pl.ds(ks, chunk), :], q, _NT, preferred_element_type=_F32)
        m_new = jnp.maximum(m, jnp.max(s_t, axis=0, keepdims=True))
        alpha = jnp.exp2((m - m_new) * exp2_scale)
        p = jnp.exp2((s_t - m_new) * exp2_scale)
        l = l * alpha + jnp.sum(p, axis=0, keepdims=True)
        acc = acc * alpha + _dot(vt_ref[c], p.astype(_BF16))
        return m_new, l, acc

    init = (jnp.full((1, tq), NEG_INF, _F32), jnp.zeros((1, tq), _F32), jnp.zeros((V_HEAD_DIM, tq), _F32))
    _, l, acc = lax.fori_loop(0, n_chunks, body, init)
    o_ref[...] = (acc / l).astype(_BF16)


def _mla(q, k, vt, n_seq, seq):
    n_tok = q.shape[0]
    tq = MLA_Q_TILE
    chunk = vt.shape[2]
    n_q = seq // tq
    n_chunks = seq // chunk
    scale = (QK_NOPE_DIM + QK_ROPE_DIM) ** -0.5
    kern = functools.partial(_mla_kernel, n_chunks=n_chunks, chunk=chunk, exp2_scale=scale * math.log2(math.e))
    return pl.pallas_call(
        kern,
        grid=(n_seq, MLA_HEADS, n_q),
        in_specs=[pl.BlockSpec((tq, MLA_HEAD_PAD), lambda b, h, i: (b * n_q + i, h)),
                  pl.BlockSpec((seq, MLA_HEAD_PAD), lambda b, h, i: (b, h)),
                  pl.BlockSpec((n_chunks, V_HEAD_DIM, chunk), lambda b, h, i: (b, h, 0))],
        out_specs=pl.BlockSpec((V_HEAD_DIM, tq), lambda b, h, i: (h, b * n_q + i)),
        out_shape=jax.ShapeDtypeStruct((MLA_OUT, n_tok), _BF16),
        compiler_params=_params("parallel", "parallel", "arbitrary"),
        name="mla_attention",
    )(q, k, vt)


def _dilated_kernel(slopes_ref, q_ref, k_ref, v_ref, o_ref, lse_ref, *, group, dilation, half, seg, heads):
    bq = min(DIL_Q_BLOCK, seg)
    kw = min(bq + 2 * half, seg)
    n_blk = seg // bq
    base = lax.broadcasted_iota(jnp.int32, (bq, kw), 1) - lax.broadcasted_iota(jnp.int32, (bq, kw), 0)
    scale = DIL_HEAD_DIM ** -0.5
    head0 = group * DIL_HEADS_PER_GROUP + pl.program_id(2) * heads

    for hd in range(heads):
        hs = slice(hd * DIL_HEAD_DIM, (hd + 1) * DIL_HEAD_DIM)
        slope = slopes_ref[head0 + hd]

        def body(i, carry, hs=hs, slope=slope):
            q0 = pl.multiple_of(i * bq, bq)
            ks = pl.multiple_of(jnp.clip(q0 - half, 0, seg - kw), half)
            q = q_ref[pl.ds(q0, bq), hs]
            k = k_ref[pl.ds(ks, kw), hs]
            v = v_ref[pl.ds(ks, kw), hs]
            s = lax.dot_general(q, k, _NT, preferred_element_type=_F32) * scale
            dist = jnp.abs(base + (ks - q0))
            alibi = -slope * (dilation * dist).astype(_F32)
            logits = jnp.where(dist <= half, s + alibi, NEG_INF)
            m = jnp.max(logits, axis=-1, keepdims=True)
            p = jnp.exp(logits - m)
            l = jnp.sum(p, axis=-1, keepdims=True)
            o = _dot(p.astype(_BF16), v) / l
            o_ref[pl.ds(q0, bq), hs] = o.astype(_BF16)
            lse_ref[pl.ds(q0, bq), hs] = jnp.broadcast_to(m + jnp.log(l), (bq, DIL_HEAD_DIM))
            return carry

        lax.fori_loop(0, n_blk, body, 0)


def _dilated(slopes, dq, dk, dv, n_seq, seq, group):
    window, dilation = DIL_PATTERNS[group]
    half = window // (2 * dilation)
    seg = seq // dilation
    n_tok = dq.shape[0]
    n_groups = len(DIL_PATTERNS)
    strided = lambda t: t.reshape(n_tok // dilation, dilation * DIL_QKV)
    heads = max(1, min(DIL_HEADS_PER_GROUP, DIL_MAX_BLOCK_ROWS // seg))
    n_hb = DIL_HEADS_PER_GROUP // heads
    width = heads * DIL_HEAD_DIM
    qkv_spec = pl.BlockSpec((seg, width), lambda b, r, j: (b, (r * n_groups + group) * n_hb + j))
    out_spec = pl.BlockSpec((seg, width), lambda b, r, j: (b, r * n_hb + j))
    kern = functools.partial(_dilated_kernel, group=group, dilation=dilation, half=half, seg=seg, heads=heads)
    o, lse = pl.pallas_call(
        kern,
        grid=(n_seq, dilation, n_hb),
        in_specs=[pl.BlockSpec(memory_space=pltpu.SMEM), qkv_spec, qkv_spec, qkv_spec],
        out_specs=(out_spec, out_spec),
        out_shape=(jax.ShapeDtypeStruct((n_tok // dilation, dilation * DIL_OUT), _BF16),
                   jax.ShapeDtypeStruct((n_tok // dilation, dilation * DIL_OUT), _F32)),
        compiler_params=_params("parallel", "parallel", "parallel"),
        name=f"dilated_g{group}",
    )(slopes, strided(dq), strided(dk), strided(dv))
    return o.reshape(n_tok, DIL_OUT), lse.reshape(n_tok, DIL_OUT)


def _merge_kernel(x_ref, omla_t_ref, o0_ref, o1_ref, o2_ref, l0_ref, l1_ref, l2_ref, gates_ref,
                  wm_ref, wd_ref, wo_ref, out_ref):
    l0, l1, l2 = l0_ref[...], l1_ref[...], l2_ref[...]
    m = jnp.maximum(jnp.maximum(l0, l1), l2)
    e0, e1, e2 = jnp.exp(l0 - m), jnp.exp(l1 - m), jnp.exp(l2 - m)
    den = e0 + e1 + e2
    o_dil = (e0 / den) * o0_ref[...].astype(_F32) + (e1 / den) * o1_ref[...].astype(_F32) \
        + (e2 / den) * o2_ref[...].astype(_F32)
    y_mla = lax.dot_general(omla_t_ref[...], wm_ref[...], _TN, preferred_element_type=_F32)
    y_dil = _dot(o_dil.astype(_BF16), wd_ref[...])
    merged = gates_ref[:, :D_MODEL].astype(_F32) * y_mla + gates_ref[:, D_MODEL:].astype(_F32) * y_dil
    out_ref[...] = x_ref[...] + _dot(merged.astype(_BF16), wo_ref[...])


def _merge(x, omla_t, dil_o, dil_lse, gates, lw):
    n_tok = x.shape[0]
    tm = TOKEN_TILE
    row = lambda w: pl.BlockSpec((tm, w), lambda i: (i, 0))
    return pl.pallas_call(
        _merge_kernel,
        grid=(n_tok // tm,),
        in_specs=[row(D_MODEL), pl.BlockSpec((MLA_OUT, tm), lambda i: (0, i)),
                  row(DIL_OUT), row(DIL_OUT), row(DIL_OUT), row(DIL_OUT), row(DIL_OUT), row(DIL_OUT),
                  row(2 * D_MODEL),
                  _resident((MLA_OUT, D_MODEL)), _resident((DIL_OUT, D_MODEL)), _resident((D_MODEL, D_MODEL))],
        out_specs=row(D_MODEL),
        out_shape=jax.ShapeDtypeStruct((n_tok, D_MODEL), _F32),
        compiler_params=_params("parallel"),
        name="merge",
    )(x, omla_t, *dil_o, *dil_lse, gates, lw["w_branch_mla"], lw["w_branch_dil"], lw["w_out"])


def _prep_layer(l, ffn1_norm, ffn1_w_gate, ffn1_w_up, ffn1_w_down, mix_norm, w_in, b_gate, q_a_norm, w_q_up,
                kv_a_norm, w_kv_up, w_branch_mla, w_branch_dil, w_out, ffn2_norm, ffn2_w_gate, ffn2_w_up,
                ffn2_w_down):
    half = QK_ROPE_DIM // 2
    w = w_in[l]
    zeros = lambda r, c: jnp.zeros((r, c), _F32)

    k_pe = w[:, Q_LORA_RANK + KV_LORA_RANK:MLA_IN]
    pad_lo, pad_hi = QK_NOPE_DIM, MLA_HEAD_PAD - QK_NOPE_DIM - QK_ROPE_DIM
    k_pe_cols = jnp.concatenate([zeros(D_MODEL, pad_lo), k_pe, zeros(D_MODEL, pad_hi)], axis=1)
    k_rot_cols = jnp.concatenate([zeros(D_MODEL, pad_lo), -k_pe[:, half:], k_pe[:, :half], zeros(D_MODEL, pad_hi)], axis=1)
    w_lat = jnp.concatenate([w[:, :Q_LORA_RANK + KV_LORA_RANK], k_pe_cols, k_rot_cols], axis=1)

    wq = w_q_up[l].reshape(Q_LORA_RANK, MLA_HEADS, QK_NOPE_DIM + QK_ROPE_DIM)
    zq = jnp.zeros((Q_LORA_RANK, MLA_HEADS, pad_hi), _F32)
    wq_pad = jnp.concatenate([wq, zq], axis=2).reshape(Q_LORA_RANK, -1)
    q1, q2 = wq[:, :, QK_NOPE_DIM:QK_NOPE_DIM + half], wq[:, :, QK_NOPE_DIM + half:]
    wq_rot = jnp.concatenate([jnp.zeros((Q_LORA_RANK, MLA_HEADS, QK_NOPE_DIM), _F32), -q2, q1, zq],
                             axis=2).reshape(Q_LORA_RANK, -1)

    wkv = w_kv_up[l].reshape(KV_LORA_RANK, MLA_HEADS, QK_NOPE_DIM + V_HEAD_DIM)
    wk = jnp.concatenate([wkv[:, :, :QK_NOPE_DIM],
                          jnp.zeros((KV_LORA_RANK, MLA_HEADS, MLA_HEAD_PAD - QK_NOPE_DIM), _F32)],
                         axis=2).reshape(KV_LORA_RANK, -1)
    wvt = wkv[:, :, QK_NOPE_DIM:].reshape(KV_LORA_RANK, MLA_OUT).T

    bf = lambda t: t.astype(_BF16)
    vec = lambda t: t.reshape(1, -1)
    return dict(
        ffn1=(vec(ffn1_norm[l]), bf(ffn1_w_gate[l]), bf(ffn1_w_up[l]), bf(ffn1_w_down[l])),
        ffn2=(vec(ffn2_norm[l]), bf(ffn2_w_gate[l]), bf(ffn2_w_up[l]), bf(ffn2_w_down[l])),
        mix_norm=vec(mix_norm[l]), w_lat=bf(w_lat), w_dil=bf(w[:, MLA_IN:MLA_IN + 3 * DIL_QKV]),
        w_gate=bf(w[:, MLA_IN + 3 * DIL_QKV:]), b_gate=vec(b_gate[l]),
        q_a_norm=vec(q_a_norm[l]), kv_a_norm=vec(kv_a_norm[l]),
        wq=bf(wq_pad), wq_rot=bf(wq_rot), wk=bf(wk), wvt=bf(wvt),
        w_branch_mla=bf(w_branch_mla[l]), w_branch_dil=bf(w_branch_dil[l]), w_out=bf(w_out[l]),
    )


def _rope_tables(seq):
    pos = jnp.arange(seq, dtype=_F32)
    inv_freq = 1.0 / (ROPE_THETA ** (jnp.arange(0, QK_ROPE_DIM, 2, dtype=_F32) / QK_ROPE_DIM))
    ang = pos[:, None] * inv_freq[None, :]
    cos, sin = jnp.cos(ang), jnp.sin(ang)
    pad_hi = MLA_HEAD_PAD - QK_NOPE_DIM - QK_ROPE_DIM
    cos_t = jnp.concatenate([jnp.ones((seq, QK_NOPE_DIM), _F32), cos, cos, jnp.ones((seq, pad_hi), _F32)], axis=1)
    sin_t = jnp.concatenate([jnp.zeros((seq, QK_NOPE_DIM), _F32), sin, sin, jnp.zeros((seq, pad_hi), _F32)], axis=1)
    return cos_t, sin_t


def _alibi_slopes(n):
    return 2.0 ** (-8.0 * jnp.arange(1, n + 1, dtype=_F32) / n)


def _trunk(x, layers, final_g):
    n_seq, seq, _ = x.shape
    x = x.reshape(n_seq * seq, D_MODEL)
    cos_t, sin_t = _rope_tables(seq)
    slopes = _alibi_slopes(DIL_HEADS)
    for l, lw in enumerate(layers):
        x = _ffn(x, *lw["ffn1"], final_g, final_norm=False)
        q, k, vt, dq, dk, dv, gates = _proj(x, seq, lw, cos_t, sin_t)
        omla_t = _mla(q, k, vt, n_seq, seq)
        dil = [_dilated(slopes, dq, dk, dv, n_seq, seq, g) for g in range(len(DIL_PATTERNS))]
        x = _merge(x, omla_t, [o for o, _ in dil], [lse for _, lse in dil], gates, lw)
        x = _ffn(x, *lw["ffn2"], final_g, final_norm=(l == len(layers) - 1))
    return x.reshape(n_seq, seq, D_MODEL)


def kernel(x_prompt, x_sample, ffn1_norm, ffn1_w_gate, ffn1_w_up, ffn1_w_down, mix_norm, w_in, b_gate, q_a_norm,
           w_q_up, kv_a_norm, w_kv_up, w_branch_mla, w_branch_dil, w_out, ffn2_norm, ffn2_w_gate, ffn2_w_up,
           ffn2_w_down, final_norm):
    stacked = (ffn1_norm, ffn1_w_gate, ffn1_w_up, ffn1_w_down, mix_norm, w_in, b_gate, q_a_norm, w_q_up, kv_a_norm,
               w_kv_up, w_branch_mla, w_branch_dil, w_out, ffn2_norm, ffn2_w_gate, ffn2_w_up, ffn2_w_down)
    layers = [_prep_layer(l, *stacked) for l in range(ffn1_norm.shape[0])]
    final_g = final_norm.reshape(1, -1)
    return (_trunk(x_prompt, layers, final_g), _trunk(x_sample, layers, final_g))
```

```python
import functools
import math

import jax
import jax.numpy as jnp
from jax import lax
from jax.experimental import pallas as pl
from jax.experimental.pallas import tpu as pltpu

D_MODEL = 1024

MLA_HEADS = 8
Q_LORA_RANK = 256
KV_LORA_RANK = 128
QK_NOPE_DIM = 64
QK_ROPE_DIM = 32
V_HEAD_DIM = 64
ROPE_THETA = 10000.0
MLA_HEAD_PAD = 128
MLA_OUT = MLA_HEADS * V_HEAD_DIM
MLA_VT_ROWS = 80
MLA_Q_SCALE = (QK_NOPE_DIM + QK_ROPE_DIM) ** -0.5 * math.log2(math.e)

DIL_PATTERNS = ((128, 1), (512, 4), (2048, 16))
DIL_HEADS_PER_GROUP = 4
DIL_HEADS = DIL_HEADS_PER_GROUP * len(DIL_PATTERNS)
DIL_HEAD_DIM = 128
DIL_OUT = DIL_HEADS_PER_GROUP * DIL_HEAD_DIM
DIL_QKV = DIL_HEADS * DIL_HEAD_DIM

D_FF = 2816
RMS_EPS = 1e-6
NEG_INF = -1e30

MLA_IN = Q_LORA_RANK + KV_LORA_RANK + QK_ROPE_DIM

V7X_VMEM_LIMIT_BYTES = 56 * 1024 * 1024

TOKEN_TILE = 512
MLA_Q_TILE = 512
DIL_Q_BLOCK = 128
DIL_MAX_BLOCK_ROWS = 4096
DIL_CHAINS = 4

_BF16 = jnp.bfloat16
_F32 = jnp.float32
_NT = (((1,), (1,)), ((), ()))
_TN = (((0,), (0,)), ((), ()))


def _params(*semantics):
    return pltpu.CompilerParams(dimension_semantics=semantics, vmem_limit_bytes=V7X_VMEM_LIMIT_BYTES)


def _resident(shape):
    return pl.BlockSpec(shape, lambda *_: (0,) * len(shape), pipeline_mode=pl.Buffered(1))


def _rms(xf, g):
    return xf * lax.rsqrt(jnp.mean(xf * xf, axis=-1, keepdims=True) + RMS_EPS) * g


def _dot(a, b):
    return jnp.dot(a, b, preferred_element_type=_F32)


def _ffn_kernel(x_ref, g_ref, wg_ref, wu_ref, wd_ref, fg_ref, o_ref, *, final_norm):
    x = x_ref[...]
    h = _rms(x, g_ref[...]).astype(_BF16)
    gate = _dot(h, wg_ref[...])
    up = _dot(h, wu_ref[...])
    act = (gate * jax.nn.sigmoid(gate) * up).astype(_BF16)
    y = x + 0.5 * _dot(act, wd_ref[...])
    if final_norm:
        y = _rms(y, fg_ref[...])
    o_ref[...] = y


def _ffn(x, norm_g, wg, wu, wd, final_g, *, final_norm):
    n_tok = x.shape[0]
    tm = TOKEN_TILE
    row = pl.BlockSpec((tm, D_MODEL), lambda i: (i, 0))
    return pl.pallas_call(
        functools.partial(_ffn_kernel, final_norm=final_norm),
        grid=(n_tok // tm,),
        in_specs=[row, _resident((1, D_MODEL)), _resident((D_MODEL, D_FF)), _resident((D_MODEL, D_FF)),
                  _resident((D_FF, D_MODEL)), _resident((1, D_MODEL))],
        out_specs=row,
        out_shape=jax.ShapeDtypeStruct((n_tok, D_MODEL), _F32),
        compiler_params=_params("parallel"),
        name="ffn",
    )(x, norm_g, wg, wu, wd, final_g)


def _proj_kernel(x_ref, g_ref, cos_ref, sin_ref, w_lat_ref, w_dil_ref, w_gate_ref, b_gate_ref, qn_ref, kvn_ref,
                 wq_ref, wq_rot_ref, wk_ref, wvt_ref, *refs):
    q_ref, k_ref, vt_ref = refs[:3]
    n_dil = 3 * len(DIL_PATTERNS)
    dil_refs = refs[3:3 + n_dil]
    gates_ref = refs[3 + n_dil]
    stage_refs = refs[4 + n_dil:]
    tm = x_ref.shape[0]

    h = _rms(x_ref[...], g_ref[...]).astype(_BF16)
    cos = cos_ref[...]
    sin = sin_ref[...]

    for j in range(3):
        for g, (_, dilation) in enumerate(DIL_PATTERNS):
            c0 = j * DIL_QKV + g * DIL_OUT
            y = _dot(h, w_dil_ref[:, c0:c0 + DIL_OUT])
            out = dil_refs[j * len(DIL_PATTERNS) + g]
            if dilation == 1:
                out[0, 0] = y.astype(_BF16)
            else:
                stage = stage_refs[g - 1]
                for c in range(DIL_HEADS_PER_GROUP):
                    cs = slice(c * DIL_HEAD_DIM, (c + 1) * DIL_HEAD_DIM)
                    stage[c] = y[:, cs]
                    for r in range(dilation):
                        out[0, r, :, cs] = stage[c, pl.ds(r, tm // dilation, stride=dilation), :].astype(_BF16)
    gates_ref[...] = jax.nn.sigmoid(_dot(h, w_gate_ref[...]) + b_gate_ref[...]).astype(_BF16)

    lat = _dot(h, w_lat_ref[...])
    c_q = _rms(lat[:, :Q_LORA_RANK], qn_ref[...]).astype(_BF16)
    c_kv = _rms(lat[:, Q_LORA_RANK:Q_LORA_RANK + KV_LORA_RANK], kvn_ref[...]).astype(_BF16)
    o = Q_LORA_RANK + KV_LORA_RANK
    k_pe = lat[:, o:o + MLA_HEAD_PAD] * cos + lat[:, o + MLA_HEAD_PAD:o + 2 * MLA_HEAD_PAD] * sin

    q = _dot(c_q, wq_ref[...])
    q_rot = _dot(c_q, wq_rot_ref[...])
    k_nope = _dot(c_kv, wk_ref[...])
    v_t = lax.dot_general(wvt_ref[...], c_kv, _NT, preferred_element_type=_F32).astype(_BF16)
    ones = jnp.ones((MLA_VT_ROWS - V_HEAD_DIM, tm), _BF16)
    for hd in range(MLA_HEADS):
        hs = slice(hd * MLA_HEAD_PAD, (hd + 1) * MLA_HEAD_PAD)
        q_ref[:, hs] = ((q[:, hs] * cos + q_rot[:, hs] * sin) * MLA_Q_SCALE).astype(_BF16)
        k_ref[:, hs] = (k_nope[:, hs] + k_pe).astype(_BF16)
        r0 = hd * MLA_VT_ROWS
        vt_ref[0, r0:r0 + V_HEAD_DIM, :] = v_t[hd * V_HEAD_DIM:(hd + 1) * V_HEAD_DIM]
        vt_ref[0, r0 + V_HEAD_DIM:r0 + MLA_VT_ROWS, :] = ones


def _residue_spec(tm, tiles_per_seq, dilation):
    return pl.BlockSpec((1, dilation, tm // dilation, DIL_OUT),
                        lambda i: (i // tiles_per_seq, 0, i % tiles_per_seq, 0))


def _proj(x, n_seq, seq, lw, cos_t, sin_t):
    n_tok = x.shape[0]
    tm = TOKEN_TILE
    tiles_per_seq = seq // tm
    row = lambda w: pl.BlockSpec((tm, w), lambda i: (i, 0))
    rope = pl.BlockSpec((tm, MLA_HEAD_PAD), lambda i: (i % tiles_per_seq, 0))
    n_lat = lw["w_lat"].shape[1]
    mla_w = MLA_HEADS * MLA_HEAD_PAD
    vt_rows = MLA_HEADS * MLA_VT_ROWS
    dil_shapes = [jax.ShapeDtypeStruct((n_seq, d, seq // d, DIL_OUT), _BF16) for _ in range(3) for _, d in DIL_PATTERNS]
    dil_specs = [_residue_spec(tm, tiles_per_seq, d) for _ in range(3) for _, d in DIL_PATTERNS]
    out_shape = [jax.ShapeDtypeStruct((n_tok, mla_w), _BF16), jax.ShapeDtypeStruct((n_tok, mla_w), _BF16),
                 jax.ShapeDtypeStruct((n_tok // tm, vt_rows, tm), _BF16), *dil_shapes,
                 jax.ShapeDtypeStruct((n_tok, 2 * D_MODEL), _BF16)]
    out_specs = [row(mla_w), row(mla_w), pl.BlockSpec((1, vt_rows, tm), lambda i: (i, 0, 0)), *dil_specs,
                 row(2 * D_MODEL)]
    outs = pl.pallas_call(
        _proj_kernel,
        grid=(n_tok // tm,),
        in_specs=[row(D_MODEL), _resident((1, D_MODEL)), rope, rope,
                  _resident((D_MODEL, n_lat)), _resident((D_MODEL, 3 * DIL_QKV)), _resident((D_MODEL, 2 * D_MODEL)),
                  _resident((1, 2 * D_MODEL)), _resident((1, Q_LORA_RANK)), _resident((1, KV_LORA_RANK)),
                  _resident((Q_LORA_RANK, mla_w)), _resident((Q_LORA_RANK, mla_w)),
                  _resident((KV_LORA_RANK, mla_w)), _resident((MLA_OUT, KV_LORA_RANK))],
        out_specs=out_specs,
        out_shape=out_shape,
        scratch_shapes=[pltpu.VMEM((DIL_HEADS_PER_GROUP, tm, DIL_HEAD_DIM), _F32) for _, d in DIL_PATTERNS if d > 1],
        compiler_params=_params("parallel"),
        name="proj",
    )(x, lw["mix_norm"], cos_t, sin_t, lw["w_lat"], lw["w_dil"], lw["w_gate"], lw["b_gate"], lw["q_a_norm"],
      lw["kv_a_norm"], lw["wq"], lw["wq_rot"], lw["wk"], lw["wvt"])
    q, k, vt = outs[:3]
    n_g = len(DIL_PATTERNS)
    dil_qkv = [tuple(outs[3 + j * n_g + g] for j in range(3)) for g in range(n_g)]
    return q, k, vt, dil_qkv, outs[-1]


def _mla_kernel(q_ref, k_ref, vt_ref, o_ref, s_ref, *, n_chunks, chunk):
    q = q_ref[...]
    tq = q.shape[0]

    def scores(c, slot):
        ks = pl.multiple_of(c * chunk, chunk)
        s_t = lax.dot_general(k_ref[pl.ds(ks, chunk), :], q, _NT, preferred_element_type=_F32)
        s_ref[slot] = s_t
        return jnp.max(s_t, axis=0, keepdims=True)

    def update(c, slot, m, m_chunk, acc):
        m_new = jnp.maximum(m, m_chunk)
        p = jnp.exp2(s_ref[slot] - m_new).astype(_BF16)
        return m_new, acc * jnp.exp2(m - m_new) + _dot(vt_ref[c], p)

    def pair(i, carry):
        m, m_chunk, acc = carry
        c = 2 * i
        m_odd = scores(c + 1, 1)
        m, acc = update(c, 0, m, m_chunk, acc)
        m_even = scores(c + 2, 0)
        m, acc = update(c + 1, 1, m, m_odd, acc)
        return m, m_even, acc

    m = jnp.full((1, tq), NEG_INF, _F32)
    acc = jnp.zeros((MLA_VT_ROWS, tq), _F32)
    m, m_chunk, acc = lax.fori_loop(0, n_chunks // 2 - 1, pair, (m, scores(0, 0), acc))
    c = n_chunks - 2
    m_odd = scores(c + 1, 1)
    m, acc = update(c, 0, m, m_chunk, acc)
    m, acc = update(c + 1, 1, m, m_odd, acc)
    o_ref[...] = (acc[:V_HEAD_DIM] / acc[V_HEAD_DIM:V_HEAD_DIM + 1]).astype(_BF16)


def _mla(q, k, vt, n_seq, seq):
    n_tok = q.shape[0]
    tq = MLA_Q_TILE
    chunk = vt.shape[2]
    n_q = seq // tq
    n_chunks = seq // chunk
    kern = functools.partial(_mla_kernel, n_chunks=n_chunks, chunk=chunk)
    return pl.pallas_call(
        kern,
        grid=(n_seq, MLA_HEADS, n_q),
        in_specs=[pl.BlockSpec((tq, MLA_HEAD_PAD), lambda b, h, i: (b * n_q + i, h)),
                  pl.BlockSpec((seq, MLA_HEAD_PAD), lambda b, h, i: (b, h)),
                  pl.BlockSpec((n_chunks, MLA_VT_ROWS, chunk), lambda b, h, i: (b, h, 0))],
        out_specs=pl.BlockSpec((V_HEAD_DIM, tq), lambda b, h, i: (h, b * n_q + i)),
        out_shape=jax.ShapeDtypeStruct((MLA_OUT, n_tok), _BF16),
        scratch_shapes=[pltpu.VMEM((2, chunk, tq), _F32)],
        compiler_params=_params("parallel", "parallel", "arbitrary"),
        name="mla_attention",
    )(q, k, vt)


def _dilated_kernel(slopes_ref, q_ref, k_ref, v_ref, o_ref, lse_ref, *, group, dilation, half, seg, heads):
    bq = min(DIL_Q_BLOCK, seg)
    kw = min(bq + 2 * half, seg)
    n_blk = seg // bq
    base = lax.broadcasted_iota(jnp.int32, (bq, kw), 1) - lax.broadcasted_iota(jnp.int32, (bq, kw), 0)
    scale = DIL_HEAD_DIM ** -0.5
    head0 = group * DIL_HEADS_PER_GROUP + pl.program_id(2) * heads
    slopes = [slopes_ref[head0 + hd] for hd in range(heads)]

    def body(i, carry):
        q0 = pl.multiple_of(i * bq, bq)
        ks = pl.multiple_of(jnp.clip(q0 - half, 0, seg - kw), half)
        dist = jnp.abs(base + (ks - q0))
        allowed = dist <= half
        token_dist = (dilation * dist).astype(_F32)
        for hd in range(heads):
            hs = slice(hd * DIL_HEAD_DIM, (hd + 1) * DIL_HEAD_DIM)
            q = q_ref[0, 0, pl.ds(q0, bq), hs]
            k = k_ref[0, 0, pl.ds(ks, kw), hs]
            v = v_ref[0, 0, pl.ds(ks, kw), hs]
            s = lax.dot_general(q, k, _NT, preferred_element_type=_F32) * scale
            logits = jnp.where(allowed, s - slopes[hd] * token_dist, NEG_INF)
            m = jnp.max(logits, axis=-1, keepdims=True)
            p = jnp.exp(logits - m)
            l = jnp.sum(p, axis=-1, keepdims=True)
            o = _dot(p.astype(_BF16), v) / l
            o_ref[0, 0, pl.ds(q0, bq), hs] = o.astype(_BF16)
            lse_ref[0, 0, pl.ds(q0, bq), hs] = jnp.broadcast_to(m + jnp.log(l), (bq, DIL_HEAD_DIM))
        return carry

    lax.fori_loop(0, n_blk, body, 0, unroll=min(n_blk, max(1, DIL_CHAINS // heads)))


def _dilated(slopes, dq, dk, dv, group):
    window, dilation = DIL_PATTERNS[group]
    n_seq, _, seg, _ = dq.shape
    half = window // (2 * dilation)
    heads = max(1, min(DIL_HEADS_PER_GROUP, DIL_MAX_BLOCK_ROWS // seg))
    n_hb = DIL_HEADS_PER_GROUP // heads
    spec = pl.BlockSpec((1, 1, seg, heads * DIL_HEAD_DIM), lambda b, r, j: (b, r, 0, j))
    kern = functools.partial(_dilated_kernel, group=group, dilation=dilation, half=half, seg=seg, heads=heads)
    return pl.pallas_call(
        kern,
        grid=(n_seq, dilation, n_hb),
        in_specs=[pl.BlockSpec(memory_space=pltpu.SMEM), spec, spec, spec],
        out_specs=(spec, spec),
        out_shape=(jax.ShapeDtypeStruct(dq.shape, _BF16), jax.ShapeDtypeStruct(dq.shape, _F32)),
        compiler_params=_params("parallel", "parallel", "parallel"),
        name=f"dilated_g{group}",
    )(slopes, dq, dk, dv)


def _merge_kernel(x_ref, omla_t_ref, o0_ref, o1_ref, o2_ref, l0_ref, l1_ref, l2_ref, gates_ref,
                  wm_ref, wd_ref, wo_ref, out_ref, *stage_refs):
    tm = x_ref.shape[0]

    def token_order(ref, stage):
        dilation = ref.shape[1]
        if dilation == 1:
            return ref[0, 0].astype(_F32)
        for c in range(DIL_HEADS_PER_GROUP):
            for r in range(dilation):
                stage[c, pl.ds(r, tm // dilation, stride=dilation), :] = \
                    ref[0, r, :, c * DIL_HEAD_DIM:(c + 1) * DIL_HEAD_DIM].astype(_F32)
        return jnp.concatenate([stage[c] for c in range(DIL_HEADS_PER_GROUP)], axis=1)

    l0 = token_order(l0_ref, None)
    l1 = token_order(l1_ref, stage_refs[0])
    l2 = token_order(l2_ref, stage_refs[1])
    m = jnp.maximum(jnp.maximum(l0, l1), l2)
    e0, e1, e2 = jnp.exp(l0 - m), jnp.exp(l1 - m), jnp.exp(l2 - m)
    den = e0 + e1 + e2
    o_dil = (e0 / den) * token_order(o0_ref, None) + (e1 / den) * token_order(o1_ref, stage_refs[2]) \
        + (e2 / den) * token_order(o2_ref, stage_refs[3])
    y_mla = lax.dot_general(omla_t_ref[...], wm_ref[...], _TN, preferred_element_type=_F32)
    y_dil = _dot(o_dil.astype(_BF16), wd_ref[...])
    merged = gates_ref[:, :D_MODEL].astype(_F32) * y_mla + gates_ref[:, D_MODEL:].astype(_F32) * y_dil
    out_ref[...] = x_ref[...] + _dot(merged.astype(_BF16), wo_ref[...])


def _merge(x, seq, omla_t, dil_o, dil_lse, gates, lw):
    n_tok = x.shape[0]
    tm = TOKEN_TILE
    tiles_per_seq = seq // tm
    row = lambda w: pl.BlockSpec((tm, w), lambda i: (i, 0))
    residue = [_residue_spec(tm, tiles_per_seq, d) for _, d in DIL_PATTERNS]
    return pl.pallas_call(
        _merge_kernel,
        grid=(n_tok // tm,),
        in_specs=[row(D_MODEL), pl.BlockSpec((MLA_OUT, tm), lambda i: (0, i)), *residue, *residue,
                  row(2 * D_MODEL),
                  _resident((MLA_OUT, D_MODEL)), _resident((DIL_OUT, D_MODEL)), _resident((D_MODEL, D_MODEL))],
        out_specs=row(D_MODEL),
        out_shape=jax.ShapeDtypeStruct((n_tok, D_MODEL), _F32),
        scratch_shapes=[pltpu.VMEM((DIL_HEADS_PER_GROUP, tm, DIL_HEAD_DIM), _F32) for _ in range(4)],
        compiler_params=_params("parallel"),
        name="merge",
    )(x, omla_t, *dil_o, *dil_lse, gates, lw["w_branch_mla"], lw["w_branch_dil"], lw["w_out"])


def _prep_layer(l, ffn1_norm, ffn1_w_gate, ffn1_w_up, ffn1_w_down, mix_norm, w_in, b_gate, q_a_norm, w_q_up,
                kv_a_norm, w_kv_up, w_branch_mla, w_branch_dil, w_out, ffn2_norm, ffn2_w_gate, ffn2_w_up,
                ffn2_w_down):
    half = QK_ROPE_DIM // 2
    w = w_in[l]
    zeros = lambda r, c: jnp.zeros((r, c), _F32)

    k_pe = w[:, Q_LORA_RANK + KV_LORA_RANK:MLA_IN]
    pad_lo, pad_hi = QK_NOPE_DIM, MLA_HEAD_PAD - QK_NOPE_DIM - QK_ROPE_DIM
    k_pe_cols = jnp.concatenate([zeros(D_MODEL, pad_lo), k_pe, zeros(D_MODEL, pad_hi)], axis=1)
    k_rot_cols = jnp.concatenate([zeros(D_MODEL, pad_lo), -k_pe[:, half:], k_pe[:, :half], zeros(D_MODEL, pad_hi)], axis=1)
    w_lat = jnp.concatenate([w[:, :Q_LORA_RANK + KV_LORA_RANK], k_pe_cols, k_rot_cols], axis=1)

    wq = w_q_up[l].reshape(Q_LORA_RANK, MLA_HEADS, QK_NOPE_DIM + QK_ROPE_DIM)
    zq = jnp.zeros((Q_LORA_RANK, MLA_HEADS, pad_hi), _F32)
    wq_pad = jnp.concatenate([wq, zq], axis=2).reshape(Q_LORA_RANK, -1)
    q1, q2 = wq[:, :, QK_NOPE_DIM:QK_NOPE_DIM + half], wq[:, :, QK_NOPE_DIM + half:]
    wq_rot = jnp.concatenate([jnp.zeros((Q_LORA_RANK, MLA_HEADS, QK_NOPE_DIM), _F32), -q2, q1, zq],
                             axis=2).reshape(Q_LORA_RANK, -1)

    wkv = w_kv_up[l].reshape(KV_LORA_RANK, MLA_HEADS, QK_NOPE_DIM + V_HEAD_DIM)
    wk = jnp.concatenate([wkv[:, :, :QK_NOPE_DIM],
                          jnp.zeros((KV_LORA_RANK, MLA_HEADS, MLA_HEAD_PAD - QK_NOPE_DIM), _F32)],
                         axis=2).reshape(KV_LORA_RANK, -1)
    wvt = wkv[:, :, QK_NOPE_DIM:].reshape(KV_LORA_RANK, MLA_OUT).T

    bf = lambda t: t.astype(_BF16)
    vec = lambda t: t.reshape(1, -1)
    return dict(
        ffn1=(vec(ffn1_norm[l]), bf(ffn1_w_gate[l]), bf(ffn1_w_up[l]), bf(ffn1_w_down[l])),
        ffn2=(vec(ffn2_norm[l]), bf(ffn2_w_gate[l]), bf(ffn2_w_up[l]), bf(ffn2_w_down[l])),
        mix_norm=vec(mix_norm[l]), w_lat=bf(w_lat), w_dil=bf(w[:, MLA_IN:MLA_IN + 3 * DIL_QKV]),
        w_gate=bf(w[:, MLA_IN + 3 * DIL_QKV:]), b_gate=vec(b_gate[l]),
        q_a_norm=vec(q_a_norm[l]), kv_a_norm=vec(kv_a_norm[l]),
        wq=bf(wq_pad), wq_rot=bf(wq_rot), wk=bf(wk), wvt=bf(wvt),
        w_branch_mla=bf(w_branch_mla[l]), w_branch_dil=bf(w_branch_dil[l]), w_out=bf(w_out[l]),
    )


def _rope_tables(seq):
    pos = jnp.arange(seq, dtype=_F32)
    inv_freq = 1.0 / (ROPE_THETA ** (jnp.arange(0, QK_ROPE_DIM, 2, dtype=_F32) / QK_ROPE_DIM))
    ang = pos[:, None] * inv_freq[None, :]
    cos, sin = jnp.cos(ang), jnp.sin(ang)
    pad_hi = MLA_HEAD_PAD - QK_NOPE_DIM - QK_ROPE_DIM
    cos_t = jnp.concatenate([jnp.ones((seq, QK_NOPE_DIM), _F32), cos, cos, jnp.ones((seq, pad_hi), _F32)], axis=1)
    sin_t = jnp.concatenate([jnp.zeros((seq, QK_NOPE_DIM), _F32), sin, sin, jnp.zeros((seq, pad_hi), _F32)], axis=1)
    return cos_t, sin_t


def _alibi_slopes(n):
    return 2.0 ** (-8.0 * jnp.arange(1, n + 1, dtype=_F32) / n)


def _trunk(x, layers, final_g):
    n_seq, seq, _ = x.shape
    x = x.reshape(n_seq * seq, D_MODEL)
    cos_t, sin_t = _rope_tables(seq)
    slopes = _alibi_slopes(DIL_HEADS)
    for l, lw in enumerate(layers):
        x = _ffn(x, *lw["ffn1"], final_g, final_norm=False)
        q, k, vt, dil_qkv, gates = _proj(x, n_seq, seq, lw, cos_t, sin_t)
        omla_t = _mla(q, k, vt, n_seq, seq)
        dil = [_dilated(slopes, *dil_qkv[g], g) for g in range(len(DIL_PATTERNS))]
        x = _merge(x, seq, omla_t, [o for o, _ in dil], [lse for _, lse in dil], gates, lw)
        x = _ffn(x, *lw["ffn2"], final_g, final_norm=(l == len(layers) - 1))
    return x.reshape(n_seq, seq, D_MODEL)


def kernel(x_prompt, x_sample, ffn1_norm, ffn1_w_gate, ffn1_w_up, ffn1_w_down, mix_norm, w_in, b_gate, q_a_norm,
           w_q_up, kv_a_norm, w_kv_up, w_branch_mla, w_branch_dil, w_out, ffn2_norm, ffn2_w_gate, ffn2_w_up,
           ffn2_w_down, final_norm):
    stacked = (ffn1_norm, ffn1_w_gate, ffn1_w_up, ffn1_w_down, mix_norm, w_in, b_gate, q_a_norm, w_q_up, kv_a_norm,
               w_kv_up, w_branch_mla, w_branch_dil, w_out, ffn2_norm, ffn2_w_gate, ffn2_w_up, ffn2_w_down)
    layers = [_prep_layer(l, *stacked) for l in range(ffn1_norm.shape[0])]
    final_g = final_norm.reshape(1, -1)
    return (_trunk(x_prompt, layers, final_g), _trunk(x_sample, layers, final_g))
```

```python
import functools
import math

import jax
import jax.numpy as jnp
from jax import lax
from jax.experimental import pallas as pl
from jax.experimental.pallas import tpu as pltpu

D_MODEL = 1024

MLA_HEADS = 8
Q_LORA_RANK = 256
KV_LORA_RANK = 128
QK_NOPE_DIM = 64
QK_ROPE_DIM = 32
V_HEAD_DIM = 64
ROPE_THETA = 10000.0
MLA_HEAD_PAD = 128
MLA_OUT = MLA_HEADS * V_HEAD_DIM
MLA_VT_ROWS = 80
MLA_Q_SCALE = (QK_NOPE_DIM + QK_ROPE_DIM) ** -0.5 * math.log2(math.e)

DIL_PATTERNS = ((128, 1), (512, 4), (2048, 16))
DIL_HEADS_PER_GROUP = 4
DIL_HEADS = DIL_HEADS_PER_GROUP * len(DIL_PATTERNS)
DIL_HEAD_DIM = 128
DIL_OUT = DIL_HEADS_PER_GROUP * DIL_HEAD_DIM
DIL_QKV = DIL_HEADS * DIL_HEAD_DIM

D_FF = 2816
RMS_EPS = 1e-6
NEG_INF = -1e30

MLA_IN = Q_LORA_RANK + KV_LORA_RANK + QK_ROPE_DIM

V7X_VMEM_LIMIT_BYTES = 56 * 1024 * 1024

TOKEN_TILE = 512
MLA_Q_TILE = 512
DIL_Q_BLOCK = 128
DIL_MAX_BLOCK_ROWS = 4096
DIL_CHAINS = 8
DIL_SCALE = DIL_HEAD_DIM ** -0.5
MLA_CHUNKS_PER_TRIP = 4

_BF16 = jnp.bfloat16
_F32 = jnp.float32
_NT = (((1,), (1,)), ((), ()))
_TN = (((0,), (0,)), ((), ()))


def _params(*semantics):
    return pltpu.CompilerParams(dimension_semantics=semantics, vmem_limit_bytes=V7X_VMEM_LIMIT_BYTES)


def _resident(shape):
    return pl.BlockSpec(shape, lambda *_: (0,) * len(shape), pipeline_mode=pl.Buffered(1))


def _rms(xf, g):
    return xf * lax.rsqrt(jnp.mean(xf * xf, axis=-1, keepdims=True) + RMS_EPS) * g


def _dot(a, b):
    return jnp.dot(a, b, preferred_element_type=_F32)


def _ffn_kernel(x_ref, g_ref, wg_ref, wu_ref, wd_ref, fg_ref, o_ref, *, final_norm):
    x = x_ref[...]
    h = _rms(x, g_ref[...]).astype(_BF16)
    gate = _dot(h, wg_ref[...])
    up = _dot(h, wu_ref[...])
    act = (gate * jax.nn.sigmoid(gate) * up).astype(_BF16)
    y = x + 0.5 * _dot(act, wd_ref[...])
    if final_norm:
        y = _rms(y, fg_ref[...])
    o_ref[...] = y


def _ffn(x, norm_g, wg, wu, wd, final_g, *, final_norm):
    n_tok = x.shape[0]
    tm = TOKEN_TILE
    row = pl.BlockSpec((tm, D_MODEL), lambda i: (i, 0))
    return pl.pallas_call(
        functools.partial(_ffn_kernel, final_norm=final_norm),
        grid=(n_tok // tm,),
        in_specs=[row, _resident((1, D_MODEL)), _resident((D_MODEL, D_FF)), _resident((D_MODEL, D_FF)),
                  _resident((D_FF, D_MODEL)), _resident((1, D_MODEL))],
        out_specs=row,
        out_shape=jax.ShapeDtypeStruct((n_tok, D_MODEL), _F32),
        compiler_params=_params("parallel"),
        name="ffn",
    )(x, norm_g, wg, wu, wd, final_g)


def _proj_kernel(x_ref, g_ref, cos_ref, sin_ref, w_lat_ref, w_dil_ref, w_gate_ref, b_gate_ref, qn_ref, kvn_ref,
                 wq_ref, wq_rot_ref, wk_ref, wvt_ref, *refs):
    q_ref, k_ref, vt_ref = refs[:3]
    n_dil = 3 * len(DIL_PATTERNS)
    dil_refs = refs[3:3 + n_dil]
    gates_ref = refs[3 + n_dil]
    stage_refs = refs[4 + n_dil:]
    tm = x_ref.shape[0]

    h = _rms(x_ref[...], g_ref[...]).astype(_BF16)
    cos = cos_ref[...]
    sin = sin_ref[...]

    for j in range(3):
        for g, (_, dilation) in enumerate(DIL_PATTERNS):
            c0 = j * DIL_QKV + g * DIL_OUT
            y = _dot(h, w_dil_ref[:, c0:c0 + DIL_OUT])
            out = dil_refs[j * len(DIL_PATTERNS) + g]
            if dilation == 1:
                out[0, 0] = y.astype(_BF16)
            else:
                stage = stage_refs[g - 1]
                for c in range(DIL_HEADS_PER_GROUP):
                    cs = slice(c * DIL_HEAD_DIM, (c + 1) * DIL_HEAD_DIM)
                    stage[c] = y[:, cs]
                    for r in range(dilation):
                        out[0, r, :, cs] = stage[c, pl.ds(r, tm // dilation, stride=dilation), :].astype(_BF16)
    gates_ref[...] = jax.nn.sigmoid(_dot(h, w_gate_ref[...]) + b_gate_ref[...]).astype(_BF16)

    lat = _dot(h, w_lat_ref[...])
    c_q = _rms(lat[:, :Q_LORA_RANK], qn_ref[...]).astype(_BF16)
    c_kv = _rms(lat[:, Q_LORA_RANK:Q_LORA_RANK + KV_LORA_RANK], kvn_ref[...]).astype(_BF16)
    o = Q_LORA_RANK + KV_LORA_RANK
    k_pe = lat[:, o:o + MLA_HEAD_PAD] * cos + lat[:, o + MLA_HEAD_PAD:o + 2 * MLA_HEAD_PAD] * sin

    q = _dot(c_q, wq_ref[...])
    q_rot = _dot(c_q, wq_rot_ref[...])
    k_nope = _dot(c_kv, wk_ref[...])
    v_t = lax.dot_general(wvt_ref[...], c_kv, _NT, preferred_element_type=_F32).astype(_BF16)
    ones = jnp.ones((MLA_VT_ROWS - V_HEAD_DIM, tm), _BF16)
    for hd in range(MLA_HEADS):
        hs = slice(hd * MLA_HEAD_PAD, (hd + 1) * MLA_HEAD_PAD)
        q_ref[:, hs] = ((q[:, hs] * cos + q_rot[:, hs] * sin) * MLA_Q_SCALE).astype(_BF16)
        k_ref[:, hs] = (k_nope[:, hs] + k_pe).astype(_BF16)
        r0 = hd * MLA_VT_ROWS
        vt_ref[0, r0:r0 + V_HEAD_DIM, :] = v_t[hd * V_HEAD_DIM:(hd + 1) * V_HEAD_DIM]
        vt_ref[0, r0 + V_HEAD_DIM:r0 + MLA_VT_ROWS, :] = ones


def _residue_spec(tm, tiles_per_seq, dilation):
    return pl.BlockSpec((1, dilation, tm // dilation, DIL_OUT),
                        lambda i: (i // tiles_per_seq, 0, i % tiles_per_seq, 0))


def _proj(x, n_seq, seq, lw, cos_t, sin_t):
    n_tok = x.shape[0]
    tm = TOKEN_TILE
    tiles_per_seq = seq // tm
    row = lambda w: pl.BlockSpec((tm, w), lambda i: (i, 0))
    rope = pl.BlockSpec((tm, MLA_HEAD_PAD), lambda i: (i % tiles_per_seq, 0))
    n_lat = lw["w_lat"].shape[1]
    mla_w = MLA_HEADS * MLA_HEAD_PAD
    vt_rows = MLA_HEADS * MLA_VT_ROWS
    dil_shapes = [jax.ShapeDtypeStruct((n_seq, d, seq // d, DIL_OUT), _BF16) for _ in range(3) for _, d in DIL_PATTERNS]
    dil_specs = [_residue_spec(tm, tiles_per_seq, d) for _ in range(3) for _, d in DIL_PATTERNS]
    out_shape = [jax.ShapeDtypeStruct((n_tok, mla_w), _BF16), jax.ShapeDtypeStruct((n_tok, mla_w), _BF16),
                 jax.ShapeDtypeStruct((n_tok // tm, vt_rows, tm), _BF16), *dil_shapes,
                 jax.ShapeDtypeStruct((n_tok, 2 * D_MODEL), _BF16)]
    out_specs = [row(mla_w), row(mla_w), pl.BlockSpec((1, vt_rows, tm), lambda i: (i, 0, 0)), *dil_specs,
                 row(2 * D_MODEL)]
    outs = pl.pallas_call(
        _proj_kernel,
        grid=(n_tok // tm,),
        in_specs=[row(D_MODEL), _resident((1, D_MODEL)), rope, rope,
                  _resident((D_MODEL, n_lat)), _resident((D_MODEL, 3 * DIL_QKV)), _resident((D_MODEL, 2 * D_MODEL)),
                  _resident((1, 2 * D_MODEL)), _resident((1, Q_LORA_RANK)), _resident((1, KV_LORA_RANK)),
                  _resident((Q_LORA_RANK, mla_w)), _resident((Q_LORA_RANK, mla_w)),
                  _resident((KV_LORA_RANK, mla_w)), _resident((MLA_OUT, KV_LORA_RANK))],
        out_specs=out_specs,
        out_shape=out_shape,
        scratch_shapes=[pltpu.VMEM((DIL_HEADS_PER_GROUP, tm, DIL_HEAD_DIM), _F32) for _, d in DIL_PATTERNS if d > 1],
        compiler_params=_params("parallel"),
        name="proj",
    )(x, lw["mix_norm"], cos_t, sin_t, lw["w_lat"], lw["w_dil"], lw["w_gate"], lw["b_gate"], lw["q_a_norm"],
      lw["kv_a_norm"], lw["wq"], lw["wq_rot"], lw["wk"], lw["wvt"])
    q, k, vt = outs[:3]
    n_g = len(DIL_PATTERNS)
    dil_qkv = [tuple(outs[3 + j * n_g + g] for j in range(3)) for g in range(n_g)]
    return q, k, vt, dil_qkv, outs[-1]


def _mla_kernel(q_ref, k_ref, vt_ref, o_ref, s_ref, *, n_q, n_chunks, chunk, tq):
    total = n_q * n_chunks
    per_trip = MLA_CHUNKS_PER_TRIP

    def scores(g, slot):
        q0 = pl.multiple_of((g // n_chunks) * tq, tq)
        k0 = pl.multiple_of((g % n_chunks) * chunk, chunk)
        s_t = lax.dot_general(k_ref[pl.ds(k0, chunk), :], q_ref[pl.ds(q0, tq), :], _NT,
                              preferred_element_type=_F32)
        s_ref[slot] = s_t
        return jnp.max(s_t, axis=0, keepdims=True)

    def update(c, slot, m, m_chunk, acc):
        m_new = jnp.maximum(m, m_chunk)
        p = jnp.exp2(s_ref[slot] - m_new).astype(_BF16)
        return m_new, acc * jnp.exp2(m - m_new) + _dot(vt_ref[c], p)

    def trip(i, carry):
        m, m_chunk, acc = carry
        g0 = i * per_trip
        c0 = g0 % n_chunks
        new_tile = c0 == 0
        m = jnp.where(new_tile, NEG_INF, m)
        acc = jnp.where(new_tile, 0.0, acc)
        for j in range(per_trip):
            m_next = scores(jnp.minimum(g0 + j + 1, total - 1), (j + 1) % per_trip)
            m, acc = update(c0 + j, j, m, m_chunk, acc)
            m_chunk = m_next
        o_ref[0, g0 // n_chunks] = (acc[:V_HEAD_DIM] / acc[V_HEAD_DIM:V_HEAD_DIM + 1]).astype(_BF16)
        return m, m_chunk, acc

    init = (jnp.full((1, tq), NEG_INF, _F32), scores(0, 0), jnp.zeros((MLA_VT_ROWS, tq), _F32))
    lax.fori_loop(0, total // per_trip, trip, init)


def _mla(q, k, vt, n_seq, seq):
    n_tok = q.shape[0]
    tq = MLA_Q_TILE
    chunk = vt.shape[2]
    n_q = seq // tq
    n_chunks = seq // chunk
    assert n_chunks % MLA_CHUNKS_PER_TRIP == 0
    kern = functools.partial(_mla_kernel, n_q=n_q, n_chunks=n_chunks, chunk=chunk, tq=tq)
    return pl.pallas_call(
        kern,
        grid=(n_seq, MLA_HEADS),
        in_specs=[pl.BlockSpec((seq, MLA_HEAD_PAD), lambda b, h: (b, h)),
                  pl.BlockSpec((seq, MLA_HEAD_PAD), lambda b, h: (b, h)),
                  pl.BlockSpec((n_chunks, MLA_VT_ROWS, chunk), lambda b, h: (b, h, 0))],
        out_specs=pl.BlockSpec((1, n_q, V_HEAD_DIM, tq), lambda b, h: (h, b, 0, 0)),
        out_shape=jax.ShapeDtypeStruct((MLA_HEADS, n_tok // tq, V_HEAD_DIM, tq), _BF16),
        scratch_shapes=[pltpu.VMEM((MLA_CHUNKS_PER_TRIP, chunk, tq), _F32)],
        compiler_params=_params("parallel", "parallel"),
        name="mla_attention",
    )(q, k, vt)


def _dilated_geometry(seg, half):
    bq = min(DIL_Q_BLOCK, seg)
    kw = min(bq + 2 * half, seg)
    return bq, kw, seg // bq


def _dilated_bias(slopes, group, seg):
    window, dilation = DIL_PATTERNS[group]
    half = window // (2 * dilation)
    bq, kw, _ = _dilated_geometry(seg, half)
    rel = jnp.arange(kw)[None, :] - jnp.arange(bq)[:, None]
    hs = slice(group * DIL_HEADS_PER_GROUP, (group + 1) * DIL_HEADS_PER_GROUP)
    tables = []
    for off in (0, -half, bq - kw):
        dist = jnp.abs(rel + off)
        alibi = -slopes[hs, None, None] * (dilation * dist).astype(_F32)[None]
        tables.append(jnp.where((dist <= half)[None], alibi / DIL_SCALE, NEG_INF))
    return jnp.stack(tables)


def _dilated_kernel(bias_ref, q_ref, k_ref, v_ref, o_ref, lse_ref, *, half, seg, heads, residues):
    bq, kw, n_blk = _dilated_geometry(seg, half)
    exp2_scale = DIL_SCALE * math.log2(math.e)

    n_units = residues * n_blk
    units_per_trip = min(n_units, max(1, DIL_CHAINS // heads))

    def body(t, carry):
        chains = []
        for j in range(units_per_trip):
            u = t * units_per_trip + j
            r = u // n_blk
            i = u % n_blk
            q0 = pl.multiple_of(i * bq, bq)
            ks = pl.multiple_of(jnp.clip(q0 - half, 0, seg - kw), half)
            variant = jnp.where(i == 0, 0, jnp.where(i == n_blk - 1, 2, 1))
            for hd in range(heads):
                chains.append((r, q0, ks, variant, hd, slice(hd * DIL_HEAD_DIM, (hd + 1) * DIL_HEAD_DIM)))
        x = jnp.concatenate(
            [lax.dot_general(q_ref[0, r, pl.ds(q0, bq), hs], k_ref[0, r, pl.ds(ks, kw), hs], _NT,
                             preferred_element_type=_F32) + bias_ref[variant, hd]
             for r, q0, ks, variant, hd, hs in chains], axis=0)
        m = jnp.max(x, axis=-1, keepdims=True)
        p = jnp.exp2((x - m) * exp2_scale)
        l = jnp.sum(p, axis=-1, keepdims=True)
        p = p.astype(_BF16)
        lse = m * DIL_SCALE + jnp.log(l)
        for c, (r, q0, ks, variant, hd, hs) in enumerate(chains):
            rows = slice(c * bq, (c + 1) * bq)
            o = _dot(p[rows], v_ref[0, r, pl.ds(ks, kw), hs]) / l[rows]
            o_ref[0, r, pl.ds(q0, bq), hs] = o.astype(_BF16)
            lse_ref[0, r, pl.ds(q0, bq), hs] = jnp.broadcast_to(lse[rows], (bq, DIL_HEAD_DIM))
        return carry

    lax.fori_loop(0, n_units // units_per_trip, body, 0)


def _dilated(bias, dq, dk, dv, group):
    window, dilation = DIL_PATTERNS[group]
    n_seq, _, seg, _ = dq.shape
    half = window // (2 * dilation)
    heads = max(1, min(DIL_HEADS_PER_GROUP, DIL_MAX_BLOCK_ROWS // seg))
    residues = max(1, min(dilation, DIL_MAX_BLOCK_ROWS // (seg * heads)))
    n_hb = DIL_HEADS_PER_GROUP // heads
    spec = pl.BlockSpec((1, residues, seg, heads * DIL_HEAD_DIM), lambda b, r, j: (b, r, 0, j))
    bias_spec = pl.BlockSpec((3, heads) + bias.shape[2:], lambda b, r, j: (0, j, 0, 0))
    kern = functools.partial(_dilated_kernel, half=half, seg=seg, heads=heads, residues=residues)
    return pl.pallas_call(
        kern,
        grid=(n_seq, dilation // residues, n_hb),
        in_specs=[bias_spec, spec, spec, spec],
        out_specs=(spec, spec),
        out_shape=(jax.ShapeDtypeStruct(dq.shape, _BF16), jax.ShapeDtypeStruct(dq.shape, _F32)),
        compiler_params=_params("parallel", "parallel", "parallel"),
        name=f"dilated_g{group}",
    )(bias, dq, dk, dv)


def _merge_kernel(x_ref, omla_t_ref, o0_ref, o1_ref, o2_ref, l0_ref, l1_ref, l2_ref, gates_ref,
                  wm_ref, wd_ref, wo_ref, out_ref, *stage_refs):
    tm = x_ref.shape[0]

    def token_order(ref, stage):
        dilation = ref.shape[1]
        if dilation == 1:
            return ref[0, 0].astype(_F32)
        for c in range(DIL_HEADS_PER_GROUP):
            for r in range(dilation):
                stage[c, pl.ds(r, tm // dilation, stride=dilation), :] = \
                    ref[0, r, :, c * DIL_HEAD_DIM:(c + 1) * DIL_HEAD_DIM].astype(_F32)
        return jnp.concatenate([stage[c] for c in range(DIL_HEADS_PER_GROUP)], axis=1)

    l0 = token_order(l0_ref, None)
    l1 = token_order(l1_ref, stage_refs[0])
    l2 = token_order(l2_ref, stage_refs[1])
    m = jnp.maximum(jnp.maximum(l0, l1), l2)
    e0, e1, e2 = jnp.exp(l0 - m), jnp.exp(l1 - m), jnp.exp(l2 - m)
    den = e0 + e1 + e2
    o_dil = (e0 / den) * token_order(o0_ref, None) + (e1 / den) * token_order(o1_ref, stage_refs[2]) \
        + (e2 / den) * token_order(o2_ref, stage_refs[3])
    omla_t = omla_t_ref[:, 0].reshape(MLA_OUT, tm)
    y_mla = lax.dot_general(omla_t, wm_ref[...], _TN, preferred_element_type=_F32)
    y_dil = _dot(o_dil.astype(_BF16), wd_ref[...])
    merged = gates_ref[:, :D_MODEL].astype(_F32) * y_mla + gates_ref[:, D_MODEL:].astype(_F32) * y_dil
    out_ref[...] = x_ref[...] + _dot(merged.astype(_BF16), wo_ref[...])


def _merge(x, seq, omla_t, dil_o, dil_lse, gates, lw):
    n_tok = x.shape[0]
    tm = TOKEN_TILE
    tiles_per_seq = seq // tm
    row = lambda w: pl.BlockSpec((tm, w), lambda i: (i, 0))
    residue = [_residue_spec(tm, tiles_per_seq, d) for _, d in DIL_PATTERNS]
    return pl.pallas_call(
        _merge_kernel,
        grid=(n_tok // tm,),
        in_specs=[row(D_MODEL), pl.BlockSpec((MLA_HEADS, 1, V_HEAD_DIM, tm), lambda i: (0, i, 0, 0)),
                  *residue, *residue,
                  row(2 * D_MODEL),
                  _resident((MLA_OUT, D_MODEL)), _resident((DIL_OUT, D_MODEL)), _resident((D_MODEL, D_MODEL))],
        out_specs=row(D_MODEL),
        out_shape=jax.ShapeDtypeStruct((n_tok, D_MODEL), _F32),
        scratch_shapes=[pltpu.VMEM((DIL_HEADS_PER_GROUP, tm, DIL_HEAD_DIM), _F32) for _ in range(4)],
        compiler_params=_params("parallel"),
        name="merge",
    )(x, omla_t, *dil_o, *dil_lse, gates, lw["w_branch_mla"], lw["w_branch_dil"], lw["w_out"])


def _prep_layer(l, ffn1_norm, ffn1_w_gate, ffn1_w_up, ffn1_w_down, mix_norm, w_in, b_gate, q_a_norm, w_q_up,
                kv_a_norm, w_kv_up, w_branch_mla, w_branch_dil, w_out, ffn2_norm, ffn2_w_gate, ffn2_w_up,
                ffn2_w_down):
    half = QK_ROPE_DIM // 2
    w = w_in[l]
    zeros = lambda r, c: jnp.zeros((r, c), _F32)

    k_pe = w[:, Q_LORA_RANK + KV_LORA_RANK:MLA_IN]
    pad_lo, pad_hi = QK_NOPE_DIM, MLA_HEAD_PAD - QK_NOPE_DIM - QK_ROPE_DIM
    k_pe_cols = jnp.concatenate([zeros(D_MODEL, pad_lo), k_pe, zeros(D_MODEL, pad_hi)], axis=1)
    k_rot_cols = jnp.concatenate([zeros(D_MODEL, pad_lo), -k_pe[:, half:], k_pe[:, :half], zeros(D_MODEL, pad_hi)], axis=1)
    w_lat = jnp.concatenate([w[:, :Q_LORA_RANK + KV_LORA_RANK], k_pe_cols, k_rot_cols], axis=1)

    wq = w_q_up[l].reshape(Q_LORA_RANK, MLA_HEADS, QK_NOPE_DIM + QK_ROPE_DIM)
    zq = jnp.zeros((Q_LORA_RANK, MLA_HEADS, pad_hi), _F32)
    wq_pad = jnp.concatenate([wq, zq], axis=2).reshape(Q_LORA_RANK, -1)
    q1, q2 = wq[:, :, QK_NOPE_DIM:QK_NOPE_DIM + half], wq[:, :, QK_NOPE_DIM + half:]
    wq_rot = jnp.concatenate([jnp.zeros((Q_LORA_RANK, MLA_HEADS, QK_NOPE_DIM), _F32), -q2, q1, zq],
                             axis=2).reshape(Q_LORA_RANK, -1)

    wkv = w_kv_up[l].reshape(KV_LORA_RANK, MLA_HEADS, QK_NOPE_DIM + V_HEAD_DIM)
    wk = jnp.concatenate([wkv[:, :, :QK_NOPE_DIM],
                          jnp.zeros((KV_LORA_RANK, MLA_HEADS, MLA_HEAD_PAD - QK_NOPE_DIM), _F32)],
                         axis=2).reshape(KV_LORA_RANK, -1)
    wvt = wkv[:, :, QK_NOPE_DIM:].reshape(KV_LORA_RANK, MLA_OUT).T

    bf = lambda t: t.astype(_BF16)
    vec = lambda t: t.reshape(1, -1)
    return dict(
        ffn1=(vec(ffn1_norm[l]), bf(ffn1_w_gate[l]), bf(ffn1_w_up[l]), bf(ffn1_w_down[l])),
        ffn2=(vec(ffn2_norm[l]), bf(ffn2_w_gate[l]), bf(ffn2_w_up[l]), bf(ffn2_w_down[l])),
        mix_norm=vec(mix_norm[l]), w_lat=bf(w_lat), w_dil=bf(w[:, MLA_IN:MLA_IN + 3 * DIL_QKV]),
        w_gate=bf(w[:, MLA_IN + 3 * DIL_QKV:]), b_gate=vec(b_gate[l]),
        q_a_norm=vec(q_a_norm[l]), kv_a_norm=vec(kv_a_norm[l]),
        wq=bf(wq_pad), wq_rot=bf(wq_rot), wk=bf(wk), wvt=bf(wvt),
        w_branch_mla=bf(w_branch_mla[l]), w_branch_dil=bf(w_branch_dil[l]), w_out=bf(w_out[l]),
    )


def _rope_tables(seq):
    pos = jnp.arange(seq, dtype=_F32)
    inv_freq = 1.0 / (ROPE_THETA ** (jnp.arange(0, QK_ROPE_DIM, 2, dtype=_F32) / QK_ROPE_DIM))
    ang = pos[:, None] * inv_freq[None, :]
    cos, sin = jnp.cos(ang), jnp.sin(ang)
    pad_hi = MLA_HEAD_PAD - QK_NOPE_DIM - QK_ROPE_DIM
    cos_t = jnp.concatenate([jnp.ones((seq, QK_NOPE_DIM), _F32), cos, cos, jnp.ones((seq, pad_hi), _F32)], axis=1)
    sin_t = jnp.concatenate([jnp.zeros((seq, QK_NOPE_DIM), _F32), sin, sin, jnp.zeros((seq, pad_hi), _F32)], axis=1)
    return cos_t, sin_t


def _alibi_slopes(n):
    return 2.0 ** (-8.0 * jnp.arange(1, n + 1, dtype=_F32) / n)


def _trunk(x, layers, final_g):
    n_seq, seq, _ = x.shape
    x = x.reshape(n_seq * seq, D_MODEL)
    cos_t, sin_t = _rope_tables(seq)
    slopes = _alibi_slopes(DIL_HEADS)
    biases = [_dilated_bias(slopes, g, seq // d) for g, (_, d) in enumerate(DIL_PATTERNS)]
    for l, lw in enumerate(layers):
        x = _ffn(x, *lw["ffn1"], final_g, final_norm=False)
        q, k, vt, dil_qkv, gates = _proj(x, n_seq, seq, lw, cos_t, sin_t)
        omla_t = _mla(q, k, vt, n_seq, seq)
        dil = [_dilated(biases[g], *dil_qkv[g], g) for g in range(len(DIL_PATTERNS))]
        x = _merge(x, seq, omla_t, [o for o, _ in dil], [lse for _, lse in dil], gates, lw)
        x = _ffn(x, *lw["ffn2"], final_g, final_norm=(l == len(layers) - 1))
    return x.reshape(n_seq, seq, D_MODEL)


def kernel(x_prompt, x_sample, ffn1_norm, ffn1_w_gate, ffn1_w_up, ffn1_w_down, mix_norm, w_in, b_gate, q_a_norm,
           w_q_up, kv_a_norm, w_kv_up, w_branch_mla, w_branch_dil, w_out, ffn2_norm, ffn2_w_gate, ffn2_w_up,
           ffn2_w_down, final_norm):
    stacked = (ffn1_norm, ffn1_w_gate, ffn1_w_up, ffn1_w_down, mix_norm, w_in, b_gate, q_a_norm, w_q_up, kv_a_norm,
               w_kv_up, w_branch_mla, w_branch_dil, w_out, ffn2_norm, ffn2_w_gate, ffn2_w_up, ffn2_w_down)
    layers = [_prep_layer(l, *stacked) for l in range(ffn1_norm.shape[0])]
    final_g = final_norm.reshape(1, -1)
    return (_trunk(x_prompt, layers, final_g), _trunk(x_sample, layers, final_g))
```

```python
import functools
import math

import jax
import jax.numpy as jnp
from jax import lax
from jax.experimental import pallas as pl
from jax.experimental.pallas import tpu as pltpu

D_MODEL = 1024

MLA_HEADS = 8
Q_LORA_RANK = 256
KV_LORA_RANK = 128
QK_NOPE_DIM = 64
QK_ROPE_DIM = 32
V_HEAD_DIM = 64
ROPE_THETA = 10000.0
MLA_HEAD_PAD = 128
MLA_OUT = MLA_HEADS * V_HEAD_DIM
MLA_VT_ROWS = 80
MLA_Q_SCALE = (QK_NOPE_DIM + QK_ROPE_DIM) ** -0.5 * math.log2(math.e)

DIL_PATTERNS = ((128, 1), (512, 4), (2048, 16))
DIL_HEADS_PER_GROUP = 4
DIL_HEADS = DIL_HEADS_PER_GROUP * len(DIL_PATTERNS)
DIL_HEAD_DIM = 128
DIL_OUT = DIL_HEADS_PER_GROUP * DIL_HEAD_DIM
DIL_QKV = DIL_HEADS * DIL_HEAD_DIM

D_FF = 2816
RMS_EPS = 1e-6
NEG_INF = -1e30

MLA_IN = Q_LORA_RANK + KV_LORA_RANK + QK_ROPE_DIM

V7X_VMEM_LIMIT_BYTES = 56 * 1024 * 1024

TOKEN_TILE = 512
MLA_Q_TILE = 512
DIL_Q_BLOCK = 128
DIL_MAX_BLOCK_ROWS = 4096
DIL_LSE_LANES = DIL_HEAD_DIM // DIL_HEADS_PER_GROUP
DIL_CHAINS = 8
DIL_SCALE = DIL_HEAD_DIM ** -0.5
MLA_CHUNKS_PER_TRIP = 8

_BF16 = jnp.bfloat16
_F32 = jnp.float32
_NT = (((1,), (1,)), ((), ()))
_TN = (((0,), (0,)), ((), ()))


def _params(*semantics):
    return pltpu.CompilerParams(dimension_semantics=semantics, vmem_limit_bytes=V7X_VMEM_LIMIT_BYTES)


def _resident(shape):
    return pl.BlockSpec(shape, lambda *_: (0,) * len(shape), pipeline_mode=pl.Buffered(1))


def _rms(xf, g):
    return xf * lax.rsqrt(jnp.mean(xf * xf, axis=-1, keepdims=True) + RMS_EPS) * g


def _dot(a, b):
    return jnp.dot(a, b, preferred_element_type=_F32)


def _ffn_kernel(x_ref, g_ref, wg_ref, wu_ref, wd_ref, fg_ref, o_ref, *, final_norm):
    x = x_ref[...]
    h = _rms(x, g_ref[...]).astype(_BF16)
    gate = _dot(h, wg_ref[...])
    up = _dot(h, wu_ref[...])
    act = (gate * jax.nn.sigmoid(gate) * up).astype(_BF16)
    y = x + 0.5 * _dot(act, wd_ref[...])
    if final_norm:
        y = _rms(y, fg_ref[...])
    o_ref[...] = y


def _ffn(x, norm_g, wg, wu, wd, final_g, *, final_norm):
    n_tok = x.shape[0]
    tm = TOKEN_TILE
    row = pl.BlockSpec((tm, D_MODEL), lambda i: (i, 0))
    return pl.pallas_call(
        functools.partial(_ffn_kernel, final_norm=final_norm),
        grid=(n_tok // tm,),
        in_specs=[row, _resident((1, D_MODEL)), _resident((D_MODEL, D_FF)), _resident((D_MODEL, D_FF)),
                  _resident((D_FF, D_MODEL)), _resident((1, D_MODEL))],
        out_specs=row,
        out_shape=jax.ShapeDtypeStruct((n_tok, D_MODEL), _F32),
        compiler_params=_params("parallel"),
        name="ffn",
    )(x, norm_g, wg, wu, wd, final_g)


def _proj_kernel(x_ref, g_ref, cos_ref, sin_ref, w_lat_ref, w_dil_ref, w_gate_ref, b_gate_ref, qn_ref, kvn_ref,
                 wq_ref, wk_ref, wvt_ref, *refs):
    q_ref, k_ref, vt_ref = refs[:3]
    n_dil = 3 * len(DIL_PATTERNS)
    dil_refs = refs[3:3 + n_dil]
    gates_ref = refs[3 + n_dil]
    stage_refs = refs[4 + n_dil:]
    tm = x_ref.shape[0]

    h = _rms(x_ref[...], g_ref[...]).astype(_BF16)
    cos = cos_ref[...]
    sin = sin_ref[...]

    for j in range(3):
        y_all = _dot(h, w_dil_ref[:, j * DIL_QKV:(j + 1) * DIL_QKV])
        for g, (_, dilation) in enumerate(DIL_PATTERNS):
            y = y_all[:, g * DIL_OUT:(g + 1) * DIL_OUT]
            out = dil_refs[j * len(DIL_PATTERNS) + g]
            if dilation == 1:
                out[0, 0] = y.astype(_BF16)
            else:
                stage = stage_refs[g - 1]
                for c in range(DIL_HEADS_PER_GROUP):
                    cs = slice(c * DIL_HEAD_DIM, (c + 1) * DIL_HEAD_DIM)
                    stage[c] = y[:, cs]
                    for r in range(dilation):
                        out[0, r, :, cs] = stage[c, pl.ds(r, tm // dilation, stride=dilation), :].astype(_BF16)
    gates_ref[...] = jax.nn.sigmoid(_dot(h, w_gate_ref[...]) + b_gate_ref[...]).astype(_BF16)

    lat = _dot(h, w_lat_ref[...])
    c_q = _rms(lat[:, :Q_LORA_RANK], qn_ref[...]).astype(_BF16)
    c_kv = _rms(lat[:, Q_LORA_RANK:Q_LORA_RANK + KV_LORA_RANK], kvn_ref[...]).astype(_BF16)
    o = Q_LORA_RANK + KV_LORA_RANK
    k_pe = lat[:, o:o + MLA_HEAD_PAD] * cos + lat[:, o + MLA_HEAD_PAD:o + 2 * MLA_HEAD_PAD] * sin

    q_both = _dot(c_q, wq_ref[...])
    q, q_rot = q_both[:, :MLA_HEADS * MLA_HEAD_PAD], q_both[:, MLA_HEADS * MLA_HEAD_PAD:]
    k_nope = _dot(c_kv, wk_ref[...])
    v_t = lax.dot_general(wvt_ref[...], c_kv, _NT, preferred_element_type=_F32).astype(_BF16)
    ones = jnp.ones((MLA_VT_ROWS - V_HEAD_DIM, tm), _BF16)
    for hd in range(MLA_HEADS):
        hs = slice(hd * MLA_HEAD_PAD, (hd + 1) * MLA_HEAD_PAD)
        q_ref[:, hs] = ((q[:, hs] * cos + q_rot[:, hs] * sin) * MLA_Q_SCALE).astype(_BF16)
        k_ref[:, hs] = (k_nope[:, hs] + k_pe).astype(_BF16)
        r0 = hd * MLA_VT_ROWS
        vt_ref[0, r0:r0 + V_HEAD_DIM, :] = v_t[hd * V_HEAD_DIM:(hd + 1) * V_HEAD_DIM]
        vt_ref[0, r0 + V_HEAD_DIM:r0 + MLA_VT_ROWS, :] = ones


def _residue_spec(tm, tiles_per_seq, dilation, width):
    return pl.BlockSpec((1, dilation, tm // dilation, width),
                        lambda i: (i // tiles_per_seq, 0, i % tiles_per_seq, 0))


def _proj(x, n_seq, seq, lw, cos_t, sin_t):
    n_tok = x.shape[0]
    tm = TOKEN_TILE
    tiles_per_seq = seq // tm
    row = lambda w: pl.BlockSpec((tm, w), lambda i: (i, 0))
    rope = pl.BlockSpec((tm, MLA_HEAD_PAD), lambda i: (i % tiles_per_seq, 0))
    n_lat = lw["w_lat"].shape[1]
    mla_w = MLA_HEADS * MLA_HEAD_PAD
    vt_rows = MLA_HEADS * MLA_VT_ROWS
    dil_shapes = [jax.ShapeDtypeStruct((n_seq, d, seq // d, DIL_OUT), _BF16) for _ in range(3) for _, d in DIL_PATTERNS]
    dil_specs = [_residue_spec(tm, tiles_per_seq, d, DIL_OUT) for _ in range(3) for _, d in DIL_PATTERNS]
    out_shape = [jax.ShapeDtypeStruct((n_tok, mla_w), _BF16), jax.ShapeDtypeStruct((n_tok, mla_w), _BF16),
                 jax.ShapeDtypeStruct((n_tok // tm, vt_rows, tm), _BF16), *dil_shapes,
                 jax.ShapeDtypeStruct((n_tok, 2 * D_MODEL), _BF16)]
    out_specs = [row(mla_w), row(mla_w), pl.BlockSpec((1, vt_rows, tm), lambda i: (i, 0, 0)), *dil_specs,
                 row(2 * D_MODEL)]
    outs = pl.pallas_call(
        _proj_kernel,
        grid=(n_tok // tm,),
        in_specs=[row(D_MODEL), _resident((1, D_MODEL)), rope, rope,
                  _resident((D_MODEL, n_lat)), _resident((D_MODEL, 3 * DIL_QKV)), _resident((D_MODEL, 2 * D_MODEL)),
                  _resident((1, 2 * D_MODEL)), _resident((1, Q_LORA_RANK)), _resident((1, KV_LORA_RANK)),
                  _resident((Q_LORA_RANK, 2 * mla_w)),
                  _resident((KV_LORA_RANK, mla_w)), _resident((MLA_OUT, KV_LORA_RANK))],
        out_specs=out_specs,
        out_shape=out_shape,
        scratch_shapes=[pltpu.VMEM((DIL_HEADS_PER_GROUP, tm, DIL_HEAD_DIM), _F32) for _, d in DIL_PATTERNS if d > 1],
        compiler_params=_params("parallel"),
        name="proj",
    )(x, lw["mix_norm"], cos_t, sin_t, lw["w_lat"], lw["w_dil"], lw["w_gate"], lw["b_gate"], lw["q_a_norm"],
      lw["kv_a_norm"], lw["wq"], lw["wk"], lw["wvt"])
    q, k, vt = outs[:3]
    n_g = len(DIL_PATTERNS)
    dil_qkv = [tuple(outs[3 + j * n_g + g] for j in range(3)) for g in range(n_g)]
    return q, k, vt, dil_qkv, outs[-1]


def _mla_kernel(q_ref, k_ref, vt_ref, o_ref, s_ref, *, n_q, n_chunks, chunk, tq, per_trip):
    total = n_q * n_chunks

    def scores(g, slot):
        q0 = pl.multiple_of((g // n_chunks) * tq, tq)
        k0 = pl.multiple_of((g % n_chunks) * chunk, chunk)
        s_t = lax.dot_general(k_ref[pl.ds(k0, chunk), :], q_ref[pl.ds(q0, tq), :], _NT,
                              preferred_element_type=_F32)
        s_ref[slot] = s_t
        return jnp.max(s_t, axis=0, keepdims=True)

    def update(c, slot, m, m_chunk, acc):
        m_new = jnp.maximum(m, m_chunk)
        p = jnp.exp2(s_ref[slot] - m_new).astype(_BF16)
        return m_new, acc * jnp.exp2(m - m_new) + _dot(vt_ref[c], p)

    def trip(i, carry):
        m, m_chunk, acc = carry
        g0 = i * per_trip
        c0 = g0 % n_chunks
        new_tile = c0 == 0
        m = jnp.where(new_tile, NEG_INF, m)
        acc = jnp.where(new_tile, 0.0, acc)
        for j in range(per_trip):
            m_next = scores(jnp.minimum(g0 + j + 1, total - 1), (j + 1) % per_trip)
            m, acc = update(c0 + j, j, m, m_chunk, acc)
            m_chunk = m_next
        o_ref[0, g0 // n_chunks] = (acc[:V_HEAD_DIM] / acc[V_HEAD_DIM:V_HEAD_DIM + 1]).astype(_BF16)
        return m, m_chunk, acc

    init = (jnp.full((1, tq), NEG_INF, _F32), scores(0, 0), jnp.zeros((MLA_VT_ROWS, tq), _F32))
    lax.fori_loop(0, total // per_trip, trip, init)


def _mla(q, k, vt, n_seq, seq):
    n_tok = q.shape[0]
    tq = MLA_Q_TILE
    chunk = vt.shape[2]
    n_q = seq // tq
    n_chunks = seq // chunk
    per_trip = min(n_chunks, MLA_CHUNKS_PER_TRIP)
    assert n_chunks % per_trip == 0
    kern = functools.partial(_mla_kernel, n_q=n_q, n_chunks=n_chunks, chunk=chunk, tq=tq, per_trip=per_trip)
    return pl.pallas_call(
        kern,
        grid=(n_seq, MLA_HEADS),
        in_specs=[pl.BlockSpec((seq, MLA_HEAD_PAD), lambda b, h: (b, h)),
                  pl.BlockSpec((seq, MLA_HEAD_PAD), lambda b, h: (b, h)),
                  pl.BlockSpec((n_chunks, MLA_VT_ROWS, chunk), lambda b, h: (b, h, 0))],
        out_specs=pl.BlockSpec((1, n_q, V_HEAD_DIM, tq), lambda b, h: (h, b, 0, 0)),
        out_shape=jax.ShapeDtypeStruct((MLA_HEADS, n_tok // tq, V_HEAD_DIM, tq), _BF16),
        scratch_shapes=[pltpu.VMEM((per_trip, chunk, tq), _F32)],
        compiler_params=_params("parallel", "parallel"),
        name="mla_attention",
    )(q, k, vt)


def _dilated_geometry(seg, half):
    bq = min(DIL_Q_BLOCK, seg)
    kw = min(bq + 2 * half, seg)
    return bq, kw, seg // bq


def _dilated_bias(slopes, group, seg):
    window, dilation = DIL_PATTERNS[group]
    half = window // (2 * dilation)
    bq, kw, _ = _dilated_geometry(seg, half)
    rel = jnp.arange(kw)[None, :] - jnp.arange(bq)[:, None]
    hs = slice(group * DIL_HEADS_PER_GROUP, (group + 1) * DIL_HEADS_PER_GROUP)
    tables = []
    for off in (0, -half, bq - kw):
        dist = jnp.abs(rel + off)
        alibi = -slopes[hs, None, None] * (dilation * dist).astype(_F32)[None]
        tables.append(jnp.where((dist <= half)[None], alibi / DIL_SCALE, NEG_INF))
    return jnp.stack(tables)


def _dilated_kernel(bias_ref, q_ref, k_ref, v_ref, o_ref, lse_ref, *, half, seg, residues):
    bq, kw, n_blk = _dilated_geometry(seg, half)
    exp2_scale = DIL_SCALE * math.log2(math.e)
    heads = DIL_HEADS_PER_GROUP
    n_units = residues * n_blk
    units_per_trip = min(n_units, max(1, DIL_CHAINS // heads))
    lane_head = lax.broadcasted_iota(jnp.int32, (bq, DIL_HEAD_DIM), 1) // DIL_LSE_LANES

    def body(t, carry):
        units = []
        for j in range(units_per_trip):
            u = t * units_per_trip + j
            r = u // n_blk
            i = u % n_blk
            q0 = pl.multiple_of(i * bq, bq)
            ks = pl.multiple_of(jnp.clip(q0 - half, 0, seg - kw), half)
            variant = jnp.where(i == 0, 0, jnp.where(i == n_blk - 1, 2, 1))
            units.append((r, q0, ks, variant))
        chains = [(r, q0, ks, variant, hd, slice(hd * DIL_HEAD_DIM, (hd + 1) * DIL_HEAD_DIM))
                  for r, q0, ks, variant in units for hd in range(heads)]
        x = jnp.concatenate(
            [lax.dot_general(q_ref[0, r, pl.ds(q0, bq), hs], k_ref[0, r, pl.ds(ks, kw), hs], _NT,
                             preferred_element_type=_F32) + bias_ref[variant, hd]
             for r, q0, ks, variant, hd, hs in chains], axis=0)
        m = jnp.max(x, axis=-1, keepdims=True)
        p = jnp.exp2((x - m) * exp2_scale)
        l = jnp.sum(p, axis=-1, keepdims=True)
        p = p.astype(_BF16)
        lse = m * DIL_SCALE + jnp.log(l)
        for c, (r, q0, ks, variant, hd, hs) in enumerate(chains):
            rows = slice(c * bq, (c + 1) * bq)
            o = _dot(p[rows], v_ref[0, r, pl.ds(ks, kw), hs]) / l[rows]
            o_ref[0, r, pl.ds(q0, bq), hs] = o.astype(_BF16)
        for j, (r, q0, ks, variant) in enumerate(units):
            tile = lse[(j * heads) * bq:(j * heads + 1) * bq]
            for hd in range(1, heads):
                tile = jnp.where(lane_head == hd, lse[(j * heads + hd) * bq:(j * heads + hd + 1) * bq], tile)
            lse_ref[0, r, pl.ds(q0, bq), :] = tile
        return carry

    lax.fori_loop(0, n_units // units_per_trip, body, 0)


def _dilated(bias, dq, dk, dv, group):
    window, dilation = DIL_PATTERNS[group]
    n_seq, _, seg, _ = dq.shape
    half = window // (2 * dilation)
    residues = max(1, min(dilation, DIL_MAX_BLOCK_ROWS // seg))
    spec = lambda w: pl.BlockSpec((1, residues, seg, w), lambda b, r: (b, r, 0, 0))
    kern = functools.partial(_dilated_kernel, half=half, seg=seg, residues=residues)
    return pl.pallas_call(
        kern,
        grid=(n_seq, dilation // residues),
        in_specs=[_resident(bias.shape), spec(DIL_OUT), spec(DIL_OUT), spec(DIL_OUT)],
        out_specs=(spec(DIL_OUT), spec(DIL_HEAD_DIM)),
        out_shape=(jax.ShapeDtypeStruct(dq.shape, _BF16),
                   jax.ShapeDtypeStruct(dq.shape[:3] + (DIL_HEAD_DIM,), _F32)),
        compiler_params=_params("parallel", "parallel"),
        name=f"dilated_g{group}",
    )(bias, dq, dk, dv)


def _merge_kernel(x_ref, omla_t_ref, o0_ref, o1_ref, o2_ref, l0_ref, l1_ref, l2_ref, gates_ref,
                  wm_ref, wd_ref, wo_ref, out_ref, *stage_refs):
    tm = x_ref.shape[0]
    lse_stages, o_stages = stage_refs[:2], stage_refs[2:]

    def lse_token_order(ref, stage):
        dilation = ref.shape[1]
        if dilation == 1:
            return ref[0, 0]
        for r in range(dilation):
            stage[pl.ds(r, tm // dilation, stride=dilation), :] = ref[0, r]
        return stage[...]

    def o_token_order(ref, stage, hd):
        dilation = ref.shape[1]
        hs = slice(hd * DIL_HEAD_DIM, (hd + 1) * DIL_HEAD_DIM)
        if dilation == 1:
            return ref[0, 0, :, hs].astype(_F32)
        for r in range(dilation):
            stage[hd, pl.ds(r, tm // dilation, stride=dilation), :] = ref[0, r, :, hs].astype(_F32)
        return stage[hd]

    l0 = lse_token_order(l0_ref, None)
    l1 = lse_token_order(l1_ref, lse_stages[0])
    l2 = lse_token_order(l2_ref, lse_stages[1])
    m = jnp.maximum(jnp.maximum(l0, l1), l2)
    e0, e1, e2 = jnp.exp(l0 - m), jnp.exp(l1 - m), jnp.exp(l2 - m)
    den = e0 + e1 + e2
    w0, w1, w2 = e0 / den, e1 / den, e2 / den
    o_heads = []
    for hd in range(DIL_HEADS_PER_GROUP):
        lane = slice(hd * DIL_LSE_LANES, hd * DIL_LSE_LANES + 1)
        o_heads.append(w0[:, lane] * o_token_order(o0_ref, None, hd)
                       + w1[:, lane] * o_token_order(o1_ref, o_stages[0], hd)
                       + w2[:, lane] * o_token_order(o2_ref, o_stages[1], hd))
    o_dil = jnp.concatenate(o_heads, axis=1)
    omla_t = omla_t_ref[:, 0].reshape(MLA_OUT, tm)
    y_mla = lax.dot_general(omla_t, wm_ref[...], _TN, preferred_element_type=_F32)
    y_dil = _dot(o_dil.astype(_BF16), wd_ref[...])
    merged = gates_ref[:, :D_MODEL].astype(_F32) * y_mla + gates_ref[:, D_MODEL:].astype(_F32) * y_dil
    out_ref[...] = x_ref[...] + _dot(merged.astype(_BF16), wo_ref[...])


def _merge(x, seq, omla_t, dil_o, dil_lse, gates, lw):
    n_tok = x.shape[0]
    tm = TOKEN_TILE
    tiles_per_seq = seq // tm
    row = lambda w: pl.BlockSpec((tm, w), lambda i: (i, 0))
    o_specs = [_residue_spec(tm, tiles_per_seq, d, DIL_OUT) for _, d in DIL_PATTERNS]
    lse_specs = [_residue_spec(tm, tiles_per_seq, d, DIL_HEAD_DIM) for _, d in DIL_PATTERNS]
    n_strided = sum(d > 1 for _, d in DIL_PATTERNS)
    return pl.pallas_call(
        _merge_kernel,
        grid=(n_tok // tm,),
        in_specs=[row(D_MODEL), pl.BlockSpec((MLA_HEADS, 1, V_HEAD_DIM, tm), lambda i: (0, i, 0, 0)),
                  *o_specs, *lse_specs, row(2 * D_MODEL),
                  _resident((MLA_OUT, D_MODEL)), _resident((DIL_OUT, D_MODEL)), _resident((D_MODEL, D_MODEL))],
        out_specs=row(D_MODEL),
        out_shape=jax.ShapeDtypeStruct((n_tok, D_MODEL), _F32),
        scratch_shapes=[pltpu.VMEM((tm, DIL_HEAD_DIM), _F32)] * n_strided
        + [pltpu.VMEM((DIL_HEADS_PER_GROUP, tm, DIL_HEAD_DIM), _F32)] * n_strided,
        compiler_params=_params("parallel"),
        name="merge",
    )(x, omla_t, *dil_o, *dil_lse, gates, lw["w_branch_mla"], lw["w_branch_dil"], lw["w_out"])


def _prep_layer(l, ffn1_norm, ffn1_w_gate, ffn1_w_up, ffn1_w_down, mix_norm, w_in, b_gate, q_a_norm, w_q_up,
                kv_a_norm, w_kv_up, w_branch_mla, w_branch_dil, w_out, ffn2_norm, ffn2_w_gate, ffn2_w_up,
                ffn2_w_down):
    half = QK_ROPE_DIM // 2
    w = w_in[l]
    zeros = lambda r, c: jnp.zeros((r, c), _F32)

    k_pe = w[:, Q_LORA_RANK + KV_LORA_RANK:MLA_IN]
    pad_lo, pad_hi = QK_NOPE_DIM, MLA_HEAD_PAD - QK_NOPE_DIM - QK_ROPE_DIM
    k_pe_cols = jnp.concatenate([zeros(D_MODEL, pad_lo), k_pe, zeros(D_MODEL, pad_hi)], axis=1)
    k_rot_cols = jnp.concatenate([zeros(D_MODEL, pad_lo), -k_pe[:, half:], k_pe[:, :half], zeros(D_MODEL, pad_hi)], axis=1)
    w_lat = jnp.concatenate([w[:, :Q_LORA_RANK + KV_LORA_RANK], k_pe_cols, k_rot_cols], axis=1)

    wq = w_q_up[l].reshape(Q_LORA_RANK, MLA_HEADS, QK_NOPE_DIM + QK_ROPE_DIM)
    zq = jnp.zeros((Q_LORA_RANK, MLA_HEADS, pad_hi), _F32)
    wq_pad = jnp.concatenate([wq, zq], axis=2).reshape(Q_LORA_RANK, -1)
    q1, q2 = wq[:, :, QK_NOPE_DIM:QK_NOPE_DIM + half], wq[:, :, QK_NOPE_DIM + half:]
    wq_rot = jnp.concatenate([jnp.zeros((Q_LORA_RANK, MLA_HEADS, QK_NOPE_DIM), _F32), -q2, q1, zq],
                             axis=2).reshape(Q_LORA_RANK, -1)

    wkv = w_kv_up[l].reshape(KV_LORA_RANK, MLA_HEADS, QK_NOPE_DIM + V_HEAD_DIM)
    wk = jnp.concatenate([wkv[:, :, :QK_NOPE_DIM],
                          jnp.zeros((KV_LORA_RANK, MLA_HEADS, MLA_HEAD_PAD - QK_NOPE_DIM), _F32)],
                         axis=2).reshape(KV_LORA_RANK, -1)
    wvt = wkv[:, :, QK_NOPE_DIM:].reshape(KV_LORA_RANK, MLA_OUT).T

    bf = lambda t: t.astype(_BF16)
    vec = lambda t: t.reshape(1, -1)
    return dict(
        ffn1=(vec(ffn1_norm[l]), bf(ffn1_w_gate[l]), bf(ffn1_w_up[l]), bf(ffn1_w_down[l])),
        ffn2=(vec(ffn2_norm[l]), bf(ffn2_w_gate[l]), bf(ffn2_w_up[l]), bf(ffn2_w_down[l])),
        mix_norm=vec(mix_norm[l]), w_lat=bf(w_lat), w_dil=bf(w[:, MLA_IN:MLA_IN + 3 * DIL_QKV]),
        w_gate=bf(w[:, MLA_IN + 3 * DIL_QKV:]), b_gate=vec(b_gate[l]),
        q_a_norm=vec(q_a_norm[l]), kv_a_norm=vec(kv_a_norm[l]),
        wq=bf(jnp.concatenate([wq_pad, wq_rot], axis=1)), wk=bf(wk), wvt=bf(wvt),
        w_branch_mla=bf(w_branch_mla[l]), w_branch_dil=bf(w_branch_dil[l]), w_out=bf(w_out[l]),
    )


def _rope_tables(seq):
    pos = jnp.arange(seq, dtype=_F32)
    inv_freq = 1.0 / (ROPE_THETA ** (jnp.arange(0, QK_ROPE_DIM, 2, dtype=_F32) / QK_ROPE_DIM))
    ang = pos[:, None] * inv_freq[None, :]
    cos, sin = jnp.cos(ang), jnp.sin(ang)
    pad_hi = MLA_HEAD_PAD - QK_NOPE_DIM - QK_ROPE_DIM
    cos_t = jnp.concatenate([jnp.ones((seq, QK_NOPE_DIM), _F32), cos, cos, jnp.ones((seq, pad_hi), _F32)], axis=1)
    sin_t = jnp.concatenate([jnp.zeros((seq, QK_NOPE_DIM), _F32), sin, sin, jnp.zeros((seq, pad_hi), _F32)], axis=1)
    return cos_t, sin_t


def _alibi_slopes(n):
    return 2.0 ** (-8.0 * jnp.arange(1, n + 1, dtype=_F32) / n)


def _trunk(x, layers, final_g):
    n_seq, seq, _ = x.shape
    x = x.reshape(n_seq * seq, D_MODEL)
    cos_t, sin_t = _rope_tables(seq)
    slopes = _alibi_slopes(DIL_HEADS)
    biases = [_dilated_bias(slopes, g, seq // d) for g, (_, d) in enumerate(DIL_PATTERNS)]
    for l, lw in enumerate(layers):
        x = _ffn(x, *lw["ffn1"], final_g, final_norm=False)
        q, k, vt, dil_qkv, gates = _proj(x, n_seq, seq, lw, cos_t, sin_t)
        omla_t = _mla(q, k, vt, n_seq, seq)
        dil = [_dilated(biases[g], *dil_qkv[g], g) for g in range(len(DIL_PATTERNS))]
        x = _merge(x, seq, omla_t, [o for o, _ in dil], [lse for _, lse in dil], gates, lw)
        x = _ffn(x, *lw["ffn2"], final_g, final_norm=(l == len(layers) - 1))
    return x.reshape(n_seq, seq, D_MODEL)


def kernel(x_prompt, x_sample, ffn1_norm, ffn1_w_gate, ffn1_w_up, ffn1_w_down, mix_norm, w_in, b_gate, q_a_norm,
           w_q_up, kv_a_norm, w_kv_up, w_branch_mla, w_branch_dil, w_out, ffn2_norm, ffn2_w_gate, ffn2_w_up,
           ffn2_w_down, final_norm):
    stacked = (ffn1_norm, ffn1_w_gate, ffn1_w_up, ffn1_w_down, mix_norm, w_in, b_gate, q_a_norm, w_q_up, kv_a_norm,
               w_kv_up, w_branch_mla, w_branch_dil, w_out, ffn2_norm, ffn2_w_gate, ffn2_w_up, ffn2_w_down)
    layers = [_prep_layer(l, *stacked) for l in range(ffn1_norm.shape[0])]
    final_g = final_norm.reshape(1, -1)
    return (_trunk(x_prompt, layers, final_g), _trunk(x_sample, layers, final_g))
```

```python
import functools
import math

import jax
import jax.numpy as jnp
from jax import lax
from jax.experimental import pallas as pl
from jax.experimental.pallas import tpu as pltpu

D_MODEL = 1024

MLA_HEADS = 8
Q_LORA_RANK = 256
KV_LORA_RANK = 128
QK_NOPE_DIM = 64
QK_ROPE_DIM = 32
V_HEAD_DIM = 64
ROPE_THETA = 10000.0
MLA_HEAD_PAD = 128
MLA_OUT = MLA_HEADS * V_HEAD_DIM
MLA_VT_ROWS = 80
MLA_Q_SCALE = (QK_NOPE_DIM + QK_ROPE_DIM) ** -0.5 * math.log2(math.e)

DIL_PATTERNS = ((128, 1), (512, 4), (2048, 16))
DIL_HEADS_PER_GROUP = 4
DIL_HEADS = DIL_HEADS_PER_GROUP * len(DIL_PATTERNS)
DIL_HEAD_DIM = 128
DIL_OUT = DIL_HEADS_PER_GROUP * DIL_HEAD_DIM
DIL_QKV = DIL_HEADS * DIL_HEAD_DIM

D_FF = 2816
RMS_EPS = 1e-6
NEG_INF = -1e30

MLA_IN = Q_LORA_RANK + KV_LORA_RANK + QK_ROPE_DIM

V7X_VMEM_LIMIT_BYTES = 56 * 1024 * 1024

TOKEN_TILE = 512
MLA_Q_TILE = 512
DIL_Q_BLOCK = 128
DIL_MAX_BLOCK_ROWS = 4096
DIL_LSE_LANES = DIL_HEAD_DIM // DIL_HEADS_PER_GROUP
DIL_CHAINS = 16
DIL_SCALE = DIL_HEAD_DIM ** -0.5
MLA_CHUNKS_PER_TRIP = 8

_BF16 = jnp.bfloat16
_F32 = jnp.float32
_NT = (((1,), (1,)), ((), ()))
_TN = (((0,), (0,)), ((), ()))


def _params(*semantics):
    return pltpu.CompilerParams(dimension_semantics=semantics, vmem_limit_bytes=V7X_VMEM_LIMIT_BYTES)


def _resident(shape):
    return pl.BlockSpec(shape, lambda *_: (0,) * len(shape), pipeline_mode=pl.Buffered(1))


def _rms(xf, g):
    return xf * lax.rsqrt(jnp.mean(xf * xf, axis=-1, keepdims=True) + RMS_EPS) * g


def _dot(a, b):
    return jnp.dot(a, b, preferred_element_type=_F32)


def _ffn_kernel(x_ref, g_ref, wg_ref, wu_ref, wd_ref, fg_ref, o_ref, *, final_norm):
    x = x_ref[...]
    h = _rms(x, g_ref[...]).astype(_BF16)
    gate = _dot(h, wg_ref[...])
    up = _dot(h, wu_ref[...])
    act = (gate * jax.nn.sigmoid(gate) * up).astype(_BF16)
    y = x + 0.5 * _dot(act, wd_ref[...])
    if final_norm:
        y = _rms(y, fg_ref[...])
    o_ref[...] = y


def _ffn(x, norm_g, wg, wu, wd, final_g, *, final_norm):
    n_tok = x.shape[0]
    tm = TOKEN_TILE
    row = pl.BlockSpec((tm, D_MODEL), lambda i: (i, 0))
    return pl.pallas_call(
        functools.partial(_ffn_kernel, final_norm=final_norm),
        grid=(n_tok // tm,),
        in_specs=[row, _resident((1, D_MODEL)), _resident((D_MODEL, D_FF)), _resident((D_MODEL, D_FF)),
                  _resident((D_FF, D_MODEL)), _resident((1, D_MODEL))],
        out_specs=row,
        out_shape=jax.ShapeDtypeStruct((n_tok, D_MODEL), _F32),
        compiler_params=_params("parallel"),
        name="ffn",
    )(x, norm_g, wg, wu, wd, final_g)


def _proj_kernel(x_ref, g_ref, cos_ref, sin_ref, cos_t_ref, sin_t_ref, w_lat_ref, w_dil_ref, qn_ref, kvn_ref,
                 wq_ref, wk_ref, wvt_ref, *refs):
    qt_ref, k_ref, vt_ref = refs[:3]
    n_dil = 3 * len(DIL_PATTERNS)
    dil_refs = refs[3:3 + n_dil]
    stage_refs = refs[3 + n_dil:]
    tm = x_ref.shape[0]

    h = _rms(x_ref[...], g_ref[...]).astype(_BF16)
    cos = cos_ref[...]
    sin = sin_ref[...]

    lat = _dot(h, w_lat_ref[...])
    y_dil = [_dot(h, w_dil_ref[:, j * DIL_QKV:(j + 1) * DIL_QKV]) for j in range(3)]
    for j in range(3):
        for g, (_, dilation) in enumerate(DIL_PATTERNS):
            y = y_dil[j][:, g * DIL_OUT:(g + 1) * DIL_OUT]
            out = dil_refs[j * len(DIL_PATTERNS) + g]
            if dilation == 1:
                out[0, 0] = y.astype(_BF16)
            else:
                stage = stage_refs[g - 1]
                for c in range(DIL_HEADS_PER_GROUP):
                    cs = slice(c * DIL_HEAD_DIM, (c + 1) * DIL_HEAD_DIM)
                    stage[c] = y[:, cs]
                    for r in range(dilation):
                        out[0, r, :, cs] = stage[c, pl.ds(r, tm // dilation, stride=dilation), :].astype(_BF16)

    c_q = _rms(lat[:, :Q_LORA_RANK], qn_ref[...]).astype(_BF16)
    c_kv = _rms(lat[:, Q_LORA_RANK:Q_LORA_RANK + KV_LORA_RANK], kvn_ref[...]).astype(_BF16)
    o = Q_LORA_RANK + KV_LORA_RANK
    k_pe = lat[:, o:o + MLA_HEAD_PAD] * cos + lat[:, o + MLA_HEAD_PAD:o + 2 * MLA_HEAD_PAD] * sin

    q_both_t = lax.dot_general(wq_ref[...], c_q, _NT, preferred_element_type=_F32)
    cos_t, sin_t = cos_t_ref[...], sin_t_ref[...]
    k_nope = _dot(c_kv, wk_ref[...])
    v_t = lax.dot_general(wvt_ref[...], c_kv, _NT, preferred_element_type=_F32).astype(_BF16)
    ones = jnp.ones((MLA_VT_ROWS - V_HEAD_DIM, tm), _BF16)
    for hd in range(MLA_HEADS):
        hs = slice(hd * MLA_HEAD_PAD, (hd + 1) * MLA_HEAD_PAD)
        q_rot_rows = slice(MLA_HEADS * MLA_HEAD_PAD + hd * MLA_HEAD_PAD, MLA_HEADS * MLA_HEAD_PAD + (hd + 1) * MLA_HEAD_PAD)
        qt_ref[0, hs, :] = ((q_both_t[hs] * cos_t + q_both_t[q_rot_rows] * sin_t) * MLA_Q_SCALE).astype(_BF16)
        k_ref[:, hs] = (k_nope[:, hs] + k_pe).astype(_BF16)
        r0 = hd * MLA_VT_ROWS
        vt_ref[0, r0:r0 + V_HEAD_DIM, :] = v_t[hd * V_HEAD_DIM:(hd + 1) * V_HEAD_DIM]
        vt_ref[0, r0 + V_HEAD_DIM:r0 + MLA_VT_ROWS, :] = ones


def _residue_spec(tm, tiles_per_seq, dilation, width):
    return pl.BlockSpec((1, dilation, tm // dilation, width),
                        lambda i: (i // tiles_per_seq, 0, i % tiles_per_seq, 0))


def _proj(x, n_seq, seq, lw, rope):
    n_tok = x.shape[0]
    tm = TOKEN_TILE
    tiles_per_seq = seq // tm
    row = lambda w: pl.BlockSpec((tm, w), lambda i: (i, 0))
    cos_r, sin_r, cos_c, sin_c = rope
    rope_rows = pl.BlockSpec((tm, MLA_HEAD_PAD), lambda i: (i % tiles_per_seq, 0))
    rope_cols = pl.BlockSpec((MLA_HEAD_PAD, tm), lambda i: (0, i % tiles_per_seq))
    n_lat = lw["w_lat"].shape[1]
    mla_w = MLA_HEADS * MLA_HEAD_PAD
    vt_rows = MLA_HEADS * MLA_VT_ROWS
    dil_shapes = [jax.ShapeDtypeStruct((n_seq, d, seq // d, DIL_OUT), _BF16) for _ in range(3) for _, d in DIL_PATTERNS]
    dil_specs = [_residue_spec(tm, tiles_per_seq, d, DIL_OUT) for _ in range(3) for _, d in DIL_PATTERNS]
    out_shape = [jax.ShapeDtypeStruct((n_tok // tm, mla_w, tm), _BF16), jax.ShapeDtypeStruct((n_tok, mla_w), _BF16),
                 jax.ShapeDtypeStruct((n_tok // tm, vt_rows, tm), _BF16), *dil_shapes]
    out_specs = [pl.BlockSpec((1, mla_w, tm), lambda i: (i, 0, 0)), row(mla_w),
                 pl.BlockSpec((1, vt_rows, tm), lambda i: (i, 0, 0)), *dil_specs]
    outs = pl.pallas_call(
        _proj_kernel,
        grid=(n_tok // tm,),
        in_specs=[row(D_MODEL), _resident((1, D_MODEL)), rope_rows, rope_rows, rope_cols, rope_cols,
                  _resident((D_MODEL, n_lat)), _resident((D_MODEL, 3 * DIL_QKV)),
                  _resident((1, Q_LORA_RANK)), _resident((1, KV_LORA_RANK)),
                  _resident((2 * mla_w, Q_LORA_RANK)),
                  _resident((KV_LORA_RANK, mla_w)), _resident((MLA_OUT, KV_LORA_RANK))],
        out_specs=out_specs,
        out_shape=out_shape,
        scratch_shapes=[pltpu.VMEM((DIL_HEADS_PER_GROUP, tm, DIL_HEAD_DIM), _F32) for _, d in DIL_PATTERNS if d > 1],
        compiler_params=_params("parallel"),
        name="proj",
    )(x, lw["mix_norm"], cos_r, sin_r, cos_c, sin_c, lw["w_lat"], lw["w_dil"], lw["q_a_norm"],
      lw["kv_a_norm"], lw["wq"], lw["wk"], lw["wvt"])
    q, k, vt = outs[:3]
    n_g = len(DIL_PATTERNS)
    dil_qkv = [tuple(outs[3 + j * n_g + g] for j in range(3)) for g in range(n_g)]
    return q, k, vt, dil_qkv


def _mla_kernel(qt_ref, k_ref, vt_ref, o_ref, s_ref, *, n_q, n_chunks, chunk, tq, per_trip):
    total = n_q * n_chunks

    def scores(g, slot):
        k0 = pl.multiple_of((g % n_chunks) * chunk, chunk)
        s_t = _dot(k_ref[pl.ds(k0, chunk), :], qt_ref[g // n_chunks])
        s_ref[slot] = s_t
        return jnp.max(s_t, axis=0, keepdims=True)

    def update(c, slot, m, m_chunk, acc):
        m_new = jnp.maximum(m, m_chunk)
        p = jnp.exp2(s_ref[slot] - m_new).astype(_BF16)
        return m_new, acc * jnp.exp2(m - m_new) + _dot(vt_ref[c], p)

    def trip(i, carry):
        m, m_chunk, acc = carry
        g0 = i * per_trip
        c0 = g0 % n_chunks
        new_tile = c0 == 0
        m = jnp.where(new_tile, NEG_INF, m)
        acc = jnp.where(new_tile, 0.0, acc)
        for j in range(per_trip):
            m_next = scores(jnp.minimum(g0 + j + 1, total - 1), (j + 1) % per_trip)
            m, acc = update(c0 + j, j, m, m_chunk, acc)
            m_chunk = m_next
        o_ref[0, g0 // n_chunks] = (acc[:V_HEAD_DIM] / acc[V_HEAD_DIM:V_HEAD_DIM + 1]).astype(_BF16)
        return m, m_chunk, acc

    init = (jnp.full((1, tq), NEG_INF, _F32), scores(0, 0), jnp.zeros((MLA_VT_ROWS, tq), _F32))
    lax.fori_loop(0, total // per_trip, trip, init)


def _mla(qt, k, vt, n_seq, seq):
    n_tok = k.shape[0]
    tq = qt.shape[2]
    chunk = vt.shape[2]
    n_q = seq // tq
    n_chunks = seq // chunk
    per_trip = min(n_chunks, MLA_CHUNKS_PER_TRIP)
    assert n_chunks % per_trip == 0
    kern = functools.partial(_mla_kernel, n_q=n_q, n_chunks=n_chunks, chunk=chunk, tq=tq, per_trip=per_trip)
    return pl.pallas_call(
        kern,
        grid=(n_seq, MLA_HEADS),
        in_specs=[pl.BlockSpec((n_q, MLA_HEAD_PAD, tq), lambda b, h: (b, h, 0)),
                  pl.BlockSpec((seq, MLA_HEAD_PAD), lambda b, h: (b, h)),
                  pl.BlockSpec((n_chunks, MLA_VT_ROWS, chunk), lambda b, h: (b, h, 0))],
        out_specs=pl.BlockSpec((1, n_q, V_HEAD_DIM, tq), lambda b, h: (h, b, 0, 0)),
        out_shape=jax.ShapeDtypeStruct((MLA_HEADS, n_tok // tq, V_HEAD_DIM, tq), _BF16),
        scratch_shapes=[pltpu.VMEM((per_trip, chunk, tq), _F32)],
        compiler_params=_params("parallel", "parallel"),
        name="mla_attention",
    )(qt, k, vt)


def _dilated_geometry(seg, half):
    bq = min(DIL_Q_BLOCK, seg)
    kw = min(bq + 2 * half, seg)
    return bq, kw, seg // bq


def _dilated_bias(slopes, group, seg):
    window, dilation = DIL_PATTERNS[group]
    half = window // (2 * dilation)
    bq, kw, _ = _dilated_geometry(seg, half)
    rel = jnp.arange(kw)[None, :] - jnp.arange(bq)[:, None]
    hs = slice(group * DIL_HEADS_PER_GROUP, (group + 1) * DIL_HEADS_PER_GROUP)
    tables = []
    for off in (0, -half, bq - kw):
        dist = jnp.abs(rel + off)
        alibi = -slopes[hs, None, None] * (dilation * dist).astype(_F32)[None]
        tables.append(jnp.where((dist <= half)[None], alibi / DIL_SCALE, NEG_INF))
    return jnp.stack(tables)


def _dilated_kernel(bias_ref, q_ref, k_ref, v_ref, o_ref, lse_ref, *, half, seg, residues):
    bq, kw, n_blk = _dilated_geometry(seg, half)
    exp2_scale = DIL_SCALE * math.log2(math.e)
    heads = DIL_HEADS_PER_GROUP
    n_units = residues * n_blk
    units_per_trip = min(n_units, max(1, DIL_CHAINS // heads))
    lane_head = lax.broadcasted_iota(jnp.int32, (bq, DIL_HEAD_DIM), 1) // DIL_LSE_LANES

    def body(t, carry):
        units = []
        for j in range(units_per_trip):
            u = t * units_per_trip + j
            r = u // n_blk
            i = u % n_blk
            q0 = pl.multiple_of(i * bq, bq)
            ks = pl.multiple_of(jnp.clip(q0 - half, 0, seg - kw), half)
            variant = jnp.where(i == 0, 0, jnp.where(i == n_blk - 1, 2, 1))
            units.append((r, q0, ks, variant))
        chains = [(r, q0, ks, variant, hd, slice(hd * DIL_HEAD_DIM, (hd + 1) * DIL_HEAD_DIM))
                  for r, q0, ks, variant in units for hd in range(heads)]
        x = jnp.concatenate(
            [lax.dot_general(q_ref[0, r, pl.ds(q0, bq), hs], k_ref[0, r, pl.ds(ks, kw), hs], _NT,
                             preferred_element_type=_F32) + bias_ref[variant, hd]
             for r, q0, ks, variant, hd, hs in chains], axis=0)
        m = jnp.max(x, axis=-1, keepdims=True)
        p = jnp.exp2((x - m) * exp2_scale)
        l = jnp.sum(p, axis=-1, keepdims=True)
        p = p.astype(_BF16)
        lse = m * DIL_SCALE + jnp.log(l)
        for c, (r, q0, ks, variant, hd, hs) in enumerate(chains):
            rows = slice(c * bq, (c + 1) * bq)
            o = _dot(p[rows], v_ref[0, r, pl.ds(ks, kw), hs]) / l[rows]
            o_ref[0, r, pl.ds(q0, bq), hs] = o.astype(_BF16)
        for j, (r, q0, ks, variant) in enumerate(units):
            tile = lse[(j * heads) * bq:(j * heads + 1) * bq]
            for hd in range(1, heads):
                tile = jnp.where(lane_head == hd, lse[(j * heads + hd) * bq:(j * heads + hd + 1) * bq], tile)
            lse_ref[0, r, pl.ds(q0, bq), :] = tile
        return carry

    lax.fori_loop(0, n_units // units_per_trip, body, 0)


def _dilated(bias, dq, dk, dv, group):
    window, dilation = DIL_PATTERNS[group]
    n_seq, _, seg, _ = dq.shape
    half = window // (2 * dilation)
    residues = max(1, min(dilation, DIL_MAX_BLOCK_ROWS // seg))
    spec = lambda w: pl.BlockSpec((1, residues, seg, w), lambda b, r: (b, r, 0, 0))
    kern = functools.partial(_dilated_kernel, half=half, seg=seg, residues=residues)
    return pl.pallas_call(
        kern,
        grid=(n_seq, dilation // residues),
        in_specs=[_resident(bias.shape), spec(DIL_OUT), spec(DIL_OUT), spec(DIL_OUT)],
        out_specs=(spec(DIL_OUT), spec(DIL_HEAD_DIM)),
        out_shape=(jax.ShapeDtypeStruct(dq.shape, _BF16),
                   jax.ShapeDtypeStruct(dq.shape[:3] + (DIL_HEAD_DIM,), _F32)),
        compiler_params=_params("parallel", "parallel"),
        name=f"dilated_g{group}",
    )(bias, dq, dk, dv)


def _merge_kernel(x_ref, omla_t_ref, o0_ref, o1_ref, o2_ref, l0_ref, l1_ref, l2_ref, g_ref, w_gate_ref, b_gate_ref,
                  wm_ref, wd_ref, wo_ref, out_ref, *stage_refs):
    tm = x_ref.shape[0]
    x = x_ref[...]
    gates = jax.nn.sigmoid(_dot(_rms(x, g_ref[...]).astype(_BF16), w_gate_ref[...]) + b_gate_ref[...])
    lse_stages, o_stages = stage_refs[:2], stage_refs[2:]

    def lse_token_order(ref, stage):
        dilation = ref.shape[1]
        if dilation == 1:
            return ref[0, 0]
        for r in range(dilation):
            stage[pl.ds(r, tm // dilation, stride=dilation), :] = ref[0, r]
        return stage[...]

    def o_token_order(ref, stage, hd):
        dilation = ref.shape[1]
        hs = slice(hd * DIL_HEAD_DIM, (hd + 1) * DIL_HEAD_DIM)
        if dilation == 1:
            return ref[0, 0, :, hs].astype(_F32)
        for r in range(dilation):
            stage[hd, pl.ds(r, tm // dilation, stride=dilation), :] = ref[0, r, :, hs].astype(_F32)
        return stage[hd]

    l0 = lse_token_order(l0_ref, None)
    l1 = lse_token_order(l1_ref, lse_stages[0])
    l2 = lse_token_order(l2_ref, lse_stages[1])
    m = jnp.maximum(jnp.maximum(l0, l1), l2)
    e0, e1, e2 = jnp.exp(l0 - m), jnp.exp(l1 - m), jnp.exp(l2 - m)
    den = e0 + e1 + e2
    w0, w1, w2 = e0 / den, e1 / den, e2 / den
    o_heads = []
    for hd in range(DIL_HEADS_PER_GROUP):
        lane = slice(hd * DIL_LSE_LANES, hd * DIL_LSE_LANES + 1)
        o_heads.append(w0[:, lane] * o_token_order(o0_ref, None, hd)
                       + w1[:, lane] * o_token_order(o1_ref, o_stages[0], hd)
                       + w2[:, lane] * o_token_order(o2_ref, o_stages[1], hd))
    o_dil = jnp.concatenate(o_heads, axis=1)
    omla_t = omla_t_ref[:, 0].reshape(MLA_OUT, tm)
    y_mla = lax.dot_general(omla_t, wm_ref[...], _TN, preferred_element_type=_F32)
    y_dil = _dot(o_dil.astype(_BF16), wd_ref[...])
    merged = gates[:, :D_MODEL] * y_mla + gates[:, D_MODEL:] * y_dil
    out_ref[...] = x + _dot(merged.astype(_BF16), wo_ref[...])


def _merge(x, seq, omla_t, dil_o, dil_lse, lw):
    n_tok = x.shape[0]
    tm = TOKEN_TILE
    tiles_per_seq = seq // tm
    row = lambda w: pl.BlockSpec((tm, w), lambda i: (i, 0))
    o_specs = [_residue_spec(tm, tiles_per_seq, d, DIL_OUT) for _, d in DIL_PATTERNS]
    lse_specs = [_residue_spec(tm, tiles_per_seq, d, DIL_HEAD_DIM) for _, d in DIL_PATTERNS]
    n_strided = sum(d > 1 for _, d in DIL_PATTERNS)
    return pl.pallas_call(
        _merge_kernel,
        grid=(n_tok // tm,),
        in_specs=[row(D_MODEL), pl.BlockSpec((MLA_HEADS, 1, V_HEAD_DIM, tm), lambda i: (0, i, 0, 0)),
                  *o_specs, *lse_specs,
                  _resident((1, D_MODEL)), _resident((D_MODEL, 2 * D_MODEL)), _resident((1, 2 * D_MODEL)),
                  _resident((MLA_OUT, D_MODEL)), _resident((DIL_OUT, D_MODEL)), _resident((D_MODEL, D_MODEL))],
        out_specs=row(D_MODEL),
        out_shape=jax.ShapeDtypeStruct((n_tok, D_MODEL), _F32),
        scratch_shapes=[pltpu.VMEM((tm, DIL_HEAD_DIM), _F32)] * n_strided
        + [pltpu.VMEM((DIL_HEADS_PER_GROUP, tm, DIL_HEAD_DIM), _F32)] * n_strided,
        compiler_params=_params("parallel"),
        name="merge",
    )(x, omla_t, *dil_o, *dil_lse, lw["mix_norm"], lw["w_gate"], lw["b_gate"], lw["w_branch_mla"], lw["w_branch_dil"], lw["w_out"])


def _prep_layer(l, ffn1_norm, ffn1_w_gate, ffn1_w_up, ffn1_w_down, mix_norm, w_in, b_gate, q_a_norm, w_q_up,
                kv_a_norm, w_kv_up, w_branch_mla, w_branch_dil, w_out, ffn2_norm, ffn2_w_gate, ffn2_w_up,
                ffn2_w_down):
    half = QK_ROPE_DIM // 2
    w = w_in[l]
    zeros = lambda r, c: jnp.zeros((r, c), _F32)

    k_pe = w[:, Q_LORA_RANK + KV_LORA_RANK:MLA_IN]
    pad_lo, pad_hi = QK_NOPE_DIM, MLA_HEAD_PAD - QK_NOPE_DIM - QK_ROPE_DIM
    k_pe_cols = jnp.concatenate([zeros(D_MODEL, pad_lo), k_pe, zeros(D_MODEL, pad_hi)], axis=1)
    k_rot_cols = jnp.concatenate([zeros(D_MODEL, pad_lo), -k_pe[:, half:], k_pe[:, :half], zeros(D_MODEL, pad_hi)], axis=1)
    w_lat = jnp.concatenate([w[:, :Q_LORA_RANK + KV_LORA_RANK], k_pe_cols, k_rot_cols], axis=1)

    wq = w_q_up[l].reshape(Q_LORA_RANK, MLA_HEADS, QK_NOPE_DIM + QK_ROPE_DIM)
    zq = jnp.zeros((Q_LORA_RANK, MLA_HEADS, pad_hi), _F32)
    wq_pad = jnp.concatenate([wq, zq], axis=2).reshape(Q_LORA_RANK, -1)
    q1, q2 = wq[:, :, QK_NOPE_DIM:QK_NOPE_DIM + half], wq[:, :, QK_NOPE_DIM + half:]
    wq_rot = jnp.concatenate([jnp.zeros((Q_LORA_RANK, MLA_HEADS, QK_NOPE_DIM), _F32), -q2, q1, zq],
                             axis=2).reshape(Q_LORA_RANK, -1)

    wkv = w_kv_up[l].reshape(KV_LORA_RANK, MLA_HEADS, QK_NOPE_DIM + V_HEAD_DIM)
    wk = jnp.concatenate([wkv[:, :, :QK_NOPE_DIM],
                          jnp.zeros((KV_LORA_RANK, MLA_HEADS, MLA_HEAD_PAD - QK_NOPE_DIM), _F32)],
                         axis=2).reshape(KV_LORA_RANK, -1)
    wvt = wkv[:, :, QK_NOPE_DIM:].reshape(KV_LORA_RANK, MLA_OUT).T

    bf = lambda t: t.astype(_BF16)
    vec = lambda t: t.reshape(1, -1)
    return dict(
        ffn1=(vec(ffn1_norm[l]), bf(ffn1_w_gate[l]), bf(ffn1_w_up[l]), bf(ffn1_w_down[l])),
        ffn2=(vec(ffn2_norm[l]), bf(ffn2_w_gate[l]), bf(ffn2_w_up[l]), bf(ffn2_w_down[l])),
        mix_norm=vec(mix_norm[l]), w_lat=bf(w_lat), w_dil=bf(w[:, MLA_IN:MLA_IN + 3 * DIL_QKV]),
        w_gate=bf(w[:, MLA_IN + 3 * DIL_QKV:]), b_gate=vec(b_gate[l]),
        q_a_norm=vec(q_a_norm[l]), kv_a_norm=vec(kv_a_norm[l]),
        wq=bf(jnp.concatenate([wq_pad, wq_rot], axis=1).T), wk=bf(wk), wvt=bf(wvt),
        w_branch_mla=bf(w_branch_mla[l]), w_branch_dil=bf(w_branch_dil[l]), w_out=bf(w_out[l]),
    )


def _rope_tables(seq):
    pos = jnp.arange(seq, dtype=_F32)
    inv_freq = 1.0 / (ROPE_THETA ** (jnp.arange(0, QK_ROPE_DIM, 2, dtype=_F32) / QK_ROPE_DIM))
    ang = pos[:, None] * inv_freq[None, :]
    cos, sin = jnp.cos(ang), jnp.sin(ang)
    pad_hi = MLA_HEAD_PAD - QK_NOPE_DIM - QK_ROPE_DIM
    cos_t = jnp.concatenate([jnp.ones((seq, QK_NOPE_DIM), _F32), cos, cos, jnp.ones((seq, pad_hi), _F32)], axis=1)
    sin_t = jnp.concatenate([jnp.zeros((seq, QK_NOPE_DIM), _F32), sin, sin, jnp.zeros((seq, pad_hi), _F32)], axis=1)
    return cos_t, sin_t


def _alibi_slopes(n):
    return 2.0 ** (-8.0 * jnp.arange(1, n + 1, dtype=_F32) / n)


def _trunk(x, layers, final_g):
    n_seq, seq, _ = x.shape
    x = x.reshape(n_seq * seq, D_MODEL)
    cos_t, sin_t = _rope_tables(seq)
    rope = (cos_t, sin_t, cos_t.T, sin_t.T)
    slopes = _alibi_slopes(DIL_HEADS)
    biases = [_dilated_bias(slopes, g, seq // d) for g, (_, d) in enumerate(DIL_PATTERNS)]
    for l, lw in enumerate(layers):
        x = _ffn(x, *lw["ffn1"], final_g, final_norm=False)
        qt, k, vt, dil_qkv = _proj(x, n_seq, seq, lw, rope)
        omla_t = _mla(qt, k, vt, n_seq, seq)
        dil = [_dilated(biases[g], *dil_qkv[g], g) for g in range(len(DIL_PATTERNS))]
        x = _merge(x, seq, omla_t, [o for o, _ in dil], [lse for _, lse in dil], lw)
        x = _ffn(x, *lw["ffn2"], final_g, final_norm=(l == len(layers) - 1))
    return x.reshape(n_seq, seq, D_MODEL)


def kernel(x_prompt, x_sample, ffn1_norm, ffn1_w_gate, ffn1_w_up, ffn1_w_down, mix_norm, w_in, b_gate, q_a_norm,
           w_q_up, kv_a_norm, w_kv_up, w_branch_mla, w_branch_dil, w_out, ffn2_norm, ffn2_w_gate, ffn2_w_up,
           ffn2_w_down, final_norm):
    stacked = (ffn1_norm, ffn1_w_gate, ffn1_w_up, ffn1_w_down, mix_norm, w_in, b_gate, q_a_norm, w_q_up, kv_a_norm,
               w_kv_up, w_branch_mla, w_branch_dil, w_out, ffn2_norm, ffn2_w_gate, ffn2_w_up, ffn2_w_down)
    layers = [_prep_layer(l, *stacked) for l in range(ffn1_norm.shape[0])]
    final_g = final_norm.reshape(1, -1)
    return (_trunk(x_prompt, layers, final_g), _trunk(x_sample, layers, final_g))
```

```python
import functools
import math

import jax
import jax.numpy as jnp
from jax import lax
from jax.experimental import pallas as pl
from jax.experimental.pallas import tpu as pltpu

D_MODEL = 1024

MLA_HEADS = 8
Q_LORA_RANK = 256
KV_LORA_RANK = 128
QK_NOPE_DIM = 64
QK_ROPE_DIM = 32
V_HEAD_DIM = 64
ROPE_THETA = 10000.0
MLA_HEAD_PAD = 128
MLA_OUT = MLA_HEADS * V_HEAD_DIM
MLA_VT_ROWS = 80
MLA_Q_SCALE = (QK_NOPE_DIM + QK_ROPE_DIM) ** -0.5 * math.log2(math.e)

DIL_PATTERNS = ((128, 1), (512, 4), (2048, 16))
DIL_HEADS_PER_GROUP = 4
DIL_HEADS = DIL_HEADS_PER_GROUP * len(DIL_PATTERNS)
DIL_HEAD_DIM = 128
DIL_OUT = DIL_HEADS_PER_GROUP * DIL_HEAD_DIM
DIL_QKV = DIL_HEADS * DIL_HEAD_DIM

D_FF = 2816
RMS_EPS = 1e-6
NEG_INF = -1e30

MLA_IN = Q_LORA_RANK + KV_LORA_RANK + QK_ROPE_DIM

V7X_VMEM_LIMIT_BYTES = 56 * 1024 * 1024

TOKEN_TILE = 512
FFN_TOKEN_TILE = 1024
FFN_SLAB = 256
MLA_Q_TILE = 512
DIL_Q_BLOCK = 128
DIL_MAX_BLOCK_ROWS = 4096
DIL_LSE_LANES = DIL_HEAD_DIM // DIL_HEADS_PER_GROUP
DIL_CHAINS = 16
DIL_SCALE = DIL_HEAD_DIM ** -0.5
MLA_CHUNKS_PER_TRIP = 8

_BF16 = jnp.bfloat16
_F32 = jnp.float32
_NT = (((1,), (1,)), ((), ()))
_TN = (((0,), (0,)), ((), ()))


def _params(*semantics):
    return pltpu.CompilerParams(dimension_semantics=semantics, vmem_limit_bytes=V7X_VMEM_LIMIT_BYTES)


def _resident(shape):
    return pl.BlockSpec(shape, lambda *_: (0,) * len(shape), pipeline_mode=pl.Buffered(1))


def _rms(xf, g):
    return xf * lax.rsqrt(jnp.mean(xf * xf, axis=-1, keepdims=True) + RMS_EPS) * g


def _dot(a, b):
    return jnp.dot(a, b, preferred_element_type=_F32)


def _ffn_kernel(x_ref, g_ref, wg_ref, wu_ref, wd_ref, fg_ref, o_ref, *, final_norm):
    x = x_ref[...]
    h = _rms(x, g_ref[...]).astype(_BF16)
    down = jnp.zeros_like(x)
    for c0 in range(0, D_FF, FFN_SLAB):
        gate = _dot(h, wg_ref[:, c0:c0 + FFN_SLAB])
        up = _dot(h, wu_ref[:, c0:c0 + FFN_SLAB])
        act = (gate * jax.nn.sigmoid(gate) * up).astype(_BF16)
        down = down + _dot(act, wd_ref[c0:c0 + FFN_SLAB, :])
    y = x + 0.5 * down
    if final_norm:
        y = _rms(y, fg_ref[...])
    o_ref[...] = y


def _ffn(x, norm_g, wg, wu, wd, final_g, *, final_norm):
    n_tok = x.shape[0]
    tm = FFN_TOKEN_TILE
    row = pl.BlockSpec((tm, D_MODEL), lambda i: (i, 0))
    return pl.pallas_call(
        functools.partial(_ffn_kernel, final_norm=final_norm),
        grid=(n_tok // tm,),
        in_specs=[row, _resident((1, D_MODEL)), _resident((D_MODEL, D_FF)), _resident((D_MODEL, D_FF)),
                  _resident((D_FF, D_MODEL)), _resident((1, D_MODEL))],
        out_specs=row,
        out_shape=jax.ShapeDtypeStruct((n_tok, D_MODEL), _F32),
        compiler_params=_params("parallel"),
        name="ffn",
    )(x, norm_g, wg, wu, wd, final_g)


def _proj_kernel(x_ref, g_ref, cos_ref, sin_ref, cos_t_ref, sin_t_ref, w_lat_ref, w_dil_ref, qn_ref, kvn_ref,
                 wq_ref, wk_ref, wvt_ref, *refs):
    qt_ref, k_ref, vt_ref = refs[:3]
    n_dil = 3 * len(DIL_PATTERNS)
    dil_refs = refs[3:3 + n_dil]
    stage_refs = refs[3 + n_dil:]
    tm = x_ref.shape[0]

    h = _rms(x_ref[...], g_ref[...]).astype(_BF16)
    cos = cos_ref[...]
    sin = sin_ref[...]

    lat = _dot(h, w_lat_ref[...])
    y_dil = [_dot(h, w_dil_ref[:, j * DIL_QKV:(j + 1) * DIL_QKV]) for j in range(3)]
    for j in range(3):
        for g, (_, dilation) in enumerate(DIL_PATTERNS):
            y = y_dil[j][:, g * DIL_OUT:(g + 1) * DIL_OUT]
            out = dil_refs[j * len(DIL_PATTERNS) + g]
            if dilation == 1:
                out[0, 0] = y.astype(_BF16)
            else:
                stage = stage_refs[g - 1]
                for c in range(DIL_HEADS_PER_GROUP):
                    cs = slice(c * DIL_HEAD_DIM, (c + 1) * DIL_HEAD_DIM)
                    stage[c] = y[:, cs]
                    for r in range(dilation):
                        out[0, r, :, cs] = stage[c, pl.ds(r, tm // dilation, stride=dilation), :].astype(_BF16)

    c_q = _rms(lat[:, :Q_LORA_RANK], qn_ref[...]).astype(_BF16)
    c_kv = _rms(lat[:, Q_LORA_RANK:Q_LORA_RANK + KV_LORA_RANK], kvn_ref[...]).astype(_BF16)
    o = Q_LORA_RANK + KV_LORA_RANK
    k_pe = lat[:, o:o + MLA_HEAD_PAD] * cos + lat[:, o + MLA_HEAD_PAD:o + 2 * MLA_HEAD_PAD] * sin

    q_both_t = lax.dot_general(wq_ref[...], c_q, _NT, preferred_element_type=_F32)
    cos_t, sin_t = cos_t_ref[...], sin_t_ref[...]
    k_nope = _dot(c_kv, wk_ref[...])
    v_t = lax.dot_general(wvt_ref[...], c_kv, _NT, preferred_element_type=_F32).astype(_BF16)
    ones = jnp.ones((MLA_VT_ROWS - V_HEAD_DIM, tm), _BF16)
    for hd in range(MLA_HEADS):
        hs = slice(hd * MLA_HEAD_PAD, (hd + 1) * MLA_HEAD_PAD)
        q_rot_rows = slice(MLA_HEADS * MLA_HEAD_PAD + hd * MLA_HEAD_PAD, MLA_HEADS * MLA_HEAD_PAD + (hd + 1) * MLA_HEAD_PAD)
        qt_ref[0, hs, :] = ((q_both_t[hs] * cos_t + q_both_t[q_rot_rows] * sin_t) * MLA_Q_SCALE).astype(_BF16)
        k_ref[:, hs] = (k_nope[:, hs] + k_pe).astype(_BF16)
        r0 = hd * MLA_VT_ROWS
        vt_ref[0, r0:r0 + V_HEAD_DIM, :] = v_t[hd * V_HEAD_DIM:(hd + 1) * V_HEAD_DIM]
        vt_ref[0, r0 + V_HEAD_DIM:r0 + MLA_VT_ROWS, :] = ones


def _residue_spec(tm, tiles_per_seq, dilation, width):
    return pl.BlockSpec((1, dilation, tm // dilation, width),
                        lambda i: (i // tiles_per_seq, 0, i % tiles_per_seq, 0))


def _proj(x, n_seq, seq, lw, rope):
    n_tok = x.shape[0]
    tm = TOKEN_TILE
    tiles_per_seq = seq // tm
    row = lambda w: pl.BlockSpec((tm, w), lambda i: (i, 0))
    cos_r, sin_r, cos_c, sin_c = rope
    rope_rows = pl.BlockSpec((tm, MLA_HEAD_PAD), lambda i: (i % tiles_per_seq, 0))
    rope_cols = pl.BlockSpec((MLA_HEAD_PAD, tm), lambda i: (0, i % tiles_per_seq))
    n_lat = lw["w_lat"].shape[1]
    mla_w = MLA_HEADS * MLA_HEAD_PAD
    vt_rows = MLA_HEADS * MLA_VT_ROWS
    dil_shapes = [jax.ShapeDtypeStruct((n_seq, d, seq // d, DIL_OUT), _BF16) for _ in range(3) for _, d in DIL_PATTERNS]
    dil_specs = [_residue_spec(tm, tiles_per_seq, d, DIL_OUT) for _ in range(3) for _, d in DIL_PATTERNS]
    out_shape = [jax.ShapeDtypeStruct((n_tok // tm, mla_w, tm), _BF16), jax.ShapeDtypeStruct((n_tok, mla_w), _BF16),
                 jax.ShapeDtypeStruct((n_tok // tm, vt_rows, tm), _BF16), *dil_shapes]
    out_specs = [pl.BlockSpec((1, mla_w, tm), lambda i: (i, 0, 0)), row(mla_w),
                 pl.BlockSpec((1, vt_rows, tm), lambda i: (i, 0, 0)), *dil_specs]
    outs = pl.pallas_call(
        _proj_kernel,
        grid=(n_tok // tm,),
        in_specs=[row(D_MODEL), _resident((1, D_MODEL)), rope_rows, rope_rows, rope_cols, rope_cols,
                  _resident((D_MODEL, n_lat)), _resident((D_MODEL, 3 * DIL_QKV)),
                  _resident((1, Q_LORA_RANK)), _resident((1, KV_LORA_RANK)),
                  _resident((2 * mla_w, Q_LORA_RANK)),
                  _resident((KV_LORA_RANK, mla_w)), _resident((MLA_OUT, KV_LORA_RANK))],
        out_specs=out_specs,
        out_shape=out_shape,
        scratch_shapes=[pltpu.VMEM((DIL_HEADS_PER_GROUP, tm, DIL_HEAD_DIM), _F32) for _, d in DIL_PATTERNS if d > 1],
        compiler_params=_params("parallel"),
        name="proj",
    )(x, lw["mix_norm"], cos_r, sin_r, cos_c, sin_c, lw["w_lat"], lw["w_dil"], lw["q_a_norm"],
      lw["kv_a_norm"], lw["wq"], lw["wk"], lw["wvt"])
    q, k, vt = outs[:3]
    n_g = len(DIL_PATTERNS)
    dil_qkv = [tuple(outs[3 + j * n_g + g] for j in range(3)) for g in range(n_g)]
    return q, k, vt, dil_qkv


def _mla_kernel(qt_ref, k_ref, vt_ref, o_ref, s_ref, *, n_q, n_chunks, chunk, tq, per_trip):
    total = n_q * n_chunks

    def scores(g, slot):
        k0 = pl.multiple_of((g % n_chunks) * chunk, chunk)
        s_t = _dot(k_ref[pl.ds(k0, chunk), :], qt_ref[g // n_chunks])
        s_ref[slot] = s_t
        return jnp.max(s_t, axis=0, keepdims=True)

    def update(c, slot, m, m_chunk, acc):
        m_new = jnp.maximum(m, m_chunk)
        p = jnp.exp2(s_ref[slot] - m_new).astype(_BF16)
        return m_new, acc * jnp.exp2(m - m_new) + _dot(vt_ref[c], p)

    def trip(i, m_chunk):
        g0 = i * per_trip
        for j in range(per_trip):
            c = j % n_chunks
            if c == 0:
                m = jnp.full((1, tq), NEG_INF, _F32)
                acc = jnp.zeros((MLA_VT_ROWS, tq), _F32)
            m_next = scores(jnp.minimum(g0 + j + 1, total - 1), (j + 1) % per_trip)
            m, acc = update(c, j, m, m_chunk, acc)
            m_chunk = m_next
            if c == n_chunks - 1:
                o_ref[0, (g0 + j) // n_chunks] = (acc[:V_HEAD_DIM] / acc[V_HEAD_DIM:V_HEAD_DIM + 1]).astype(_BF16)
        return m_chunk

    lax.fori_loop(0, total // per_trip, trip, scores(0, 0))


def _mla(qt, k, vt, n_seq, seq):
    n_tok = k.shape[0]
    tq = qt.shape[2]
    chunk = vt.shape[2]
    n_q = seq // tq
    n_chunks = seq // chunk
    per_trip = MLA_CHUNKS_PER_TRIP
    assert per_trip % n_chunks == 0 and (n_q * n_chunks) % per_trip == 0
    kern = functools.partial(_mla_kernel, n_q=n_q, n_chunks=n_chunks, chunk=chunk, tq=tq, per_trip=per_trip)
    return pl.pallas_call(
        kern,
        grid=(n_seq, MLA_HEADS),
        in_specs=[pl.BlockSpec((n_q, MLA_HEAD_PAD, tq), lambda b, h: (b, h, 0)),
                  pl.BlockSpec((seq, MLA_HEAD_PAD), lambda b, h: (b, h)),
                  pl.BlockSpec((n_chunks, MLA_VT_ROWS, chunk), lambda b, h: (b, h, 0))],
        out_specs=pl.BlockSpec((1, n_q, V_HEAD_DIM, tq), lambda b, h: (h, b, 0, 0)),
        out_shape=jax.ShapeDtypeStruct((MLA_HEADS, n_tok // tq, V_HEAD_DIM, tq), _BF16),
        scratch_shapes=[pltpu.VMEM((per_trip, chunk, tq), _F32)],
        compiler_params=_params("parallel", "parallel"),
        name="mla_attention",
    )(qt, k, vt)


def _dilated_geometry(seg, half):
    bq = min(DIL_Q_BLOCK, seg)
    kw = min(bq + 2 * half, seg)
    return bq, kw, seg // bq


def _dilated_bias(slopes, group, seg):
    window, dilation = DIL_PATTERNS[group]
    half = window // (2 * dilation)
    bq, kw, _ = _dilated_geometry(seg, half)
    rel = jnp.arange(kw)[None, :] - jnp.arange(bq)[:, None]
    hs = slice(group * DIL_HEADS_PER_GROUP, (group + 1) * DIL_HEADS_PER_GROUP)
    tables = []
    for off in (0, -half, bq - kw):
        dist = jnp.abs(rel + off)
        alibi = -slopes[hs, None, None] * (dilation * dist).astype(_F32)[None]
        tables.append(jnp.where((dist <= half)[None], alibi / DIL_SCALE, NEG_INF))
    return jnp.stack(tables)


def _dilated_kernel(bias_ref, q_ref, k_ref, v_ref, o_ref, lse_ref, *, half, seg, residues):
    bq, kw, n_blk = _dilated_geometry(seg, half)
    exp2_scale = DIL_SCALE * math.log2(math.e)
    heads = DIL_HEADS_PER_GROUP
    n_units = residues * n_blk
    units_per_trip = min(n_units, max(1, DIL_CHAINS // heads))
    lane_head = lax.broadcasted_iota(jnp.int32, (bq, DIL_HEAD_DIM), 1) // DIL_LSE_LANES

    def body(t, carry):
        units = []
        for j in range(units_per_trip):
            u = t * units_per_trip + j
            r = u // n_blk
            i = u % n_blk
            q0 = pl.multiple_of(i * bq, bq)
            ks = pl.multiple_of(jnp.clip(q0 - half, 0, seg - kw), half)
            variant = jnp.where(i == 0, 0, jnp.where(i == n_blk - 1, 2, 1))
            units.append((r, q0, ks, variant))
        chains = [(r, q0, ks, variant, hd, slice(hd * DIL_HEAD_DIM, (hd + 1) * DIL_HEAD_DIM))
                  for r, q0, ks, variant in units for hd in range(heads)]
        x = jnp.concatenate(
            [lax.dot_general(q_ref[0, r, pl.ds(q0, bq), hs], k_ref[0, r, pl.ds(ks, kw), hs], _NT,
                             preferred_element_type=_F32) + bias_ref[variant, hd]
             for r, q0, ks, variant, hd, hs in chains], axis=0)
        m = jnp.max(x, axis=-1, keepdims=True)
        p = jnp.exp2((x - m) * exp2_scale)
        l = jnp.sum(p, axis=-1, keepdims=True)
        p = p.astype(_BF16)
        lse = m * DIL_SCALE + jnp.log(l)
        for c, (r, q0, ks, variant, hd, hs) in enumerate(chains):
            rows = slice(c * bq, (c + 1) * bq)
            o = _dot(p[rows], v_ref[0, r, pl.ds(ks, kw), hs]) / l[rows]
            o_ref[0, r, pl.ds(q0, bq), hs] = o.astype(_BF16)
        for j, (r, q0, ks, variant) in enumerate(units):
            tile = lse[(j * heads) * bq:(j * heads + 1) * bq]
            for hd in range(1, heads):
                tile = jnp.where(lane_head == hd, lse[(j * heads + hd) * bq:(j * heads + hd + 1) * bq], tile)
            lse_ref[0, r, pl.ds(q0, bq), :] = tile
        return carry

    lax.fori_loop(0, n_units // units_per_trip, body, 0)


def _dilated(bias, dq, dk, dv, group):
    window, dilation = DIL_PATTERNS[group]
    n_seq, _, seg, _ = dq.shape
    half = window // (2 * dilation)
    residues = max(1, min(dilation, DIL_MAX_BLOCK_ROWS // seg))
    spec = lambda w: pl.BlockSpec((1, residues, seg, w), lambda b, r: (b, r, 0, 0))
    kern = functools.partial(_dilated_kernel, half=half, seg=seg, residues=residues)
    return pl.pallas_call(
        kern,
        grid=(n_seq, dilation // residues),
        in_specs=[_resident(bias.shape), spec(DIL_OUT), spec(DIL_OUT), spec(DIL_OUT)],
        out_specs=(spec(DIL_OUT), spec(DIL_HEAD_DIM)),
        out_shape=(jax.ShapeDtypeStruct(dq.shape, _BF16),
                   jax.ShapeDtypeStruct(dq.shape[:3] + (DIL_HEAD_DIM,), _F32)),
        compiler_params=_params("parallel", "parallel"),
        name=f"dilated_g{group}",
    )(bias, dq, dk, dv)


def _merge_kernel(x_ref, omla_t_ref, o0_ref, o1_ref, o2_ref, l0_ref, l1_ref, l2_ref, g_ref, w_gate_ref, b_gate_ref,
                  wm_ref, wd_ref, wo_ref, out_ref, *stage_refs):
    tm = x_ref.shape[0]
    x = x_ref[...]
    gates = jax.nn.sigmoid(_dot(_rms(x, g_ref[...]).astype(_BF16), w_gate_ref[...]) + b_gate_ref[...])
    lse_stages, o_stages = stage_refs[:2], stage_refs[2:]

    def lse_token_order(ref, stage):
        dilation = ref.shape[1]
        if dilation == 1:
            return ref[0, 0]
        for r in range(dilation):
            stage[pl.ds(r, tm // dilation, stride=dilation), :] = ref[0, r]
        return stage[...]

    def o_token_order(ref, stage, hd):
        dilation = ref.shape[1]
        hs = slice(hd * DIL_HEAD_DIM, (hd + 1) * DIL_HEAD_DIM)
        if dilation == 1:
            return ref[0, 0, :, hs].astype(_F32)
        for r in range(dilation):
            stage[hd, pl.ds(r, tm // dilation, stride=dilation), :] = ref[0, r, :, hs].astype(_F32)
        return stage[hd]

    l0 = lse_token_order(l0_ref, None)
    l1 = lse_token_order(l1_ref, lse_stages[0])
    l2 = lse_token_order(l2_ref, lse_stages[1])
    m = jnp.maximum(jnp.maximum(l0, l1), l2)
    e0, e1, e2 = jnp.exp(l0 - m), jnp.exp(l1 - m), jnp.exp(l2 - m)
    den = e0 + e1 + e2
    w0, w1, w2 = e0 / den, e1 / den, e2 / den
    o_heads = []
    for hd in range(DIL_HEADS_PER_GROUP):
        lane = slice(hd * DIL_LSE_LANES, hd * DIL_LSE_LANES + 1)
        o_heads.append(w0[:, lane] * o_token_order(o0_ref, None, hd)
                       + w1[:, lane] * o_token_order(o1_ref, o_stages[0], hd)
                       + w2[:, lane] * o_token_order(o2_ref, o_stages[1], hd))
    o_dil = jnp.concatenate(o_heads, axis=1)
    omla_t = omla_t_ref[:, 0].reshape(MLA_OUT, tm)
    y_mla = lax.dot_general(omla_t, wm_ref[...], _TN, preferred_element_type=_F32)
    y_dil = _dot(o_dil.astype(_BF16), wd_ref[...])
    merged = gates[:, :D_MODEL] * y_mla + gates[:, D_MODEL:] * y_dil
    out_ref[...] = x + _dot(merged.astype(_BF16), wo_ref[...])


def _merge(x, seq, omla_t, dil_o, dil_lse, lw):
    n_tok = x.shape[0]
    tm = TOKEN_TILE
    tiles_per_seq = seq // tm
    row = lambda w: pl.BlockSpec((tm, w), lambda i: (i, 0))
    o_specs = [_residue_spec(tm, tiles_per_seq, d, DIL_OUT) for _, d in DIL_PATTERNS]
    lse_specs = [_residue_spec(tm, tiles_per_seq, d, DIL_HEAD_DIM) for _, d in DIL_PATTERNS]
    n_strided = sum(d > 1 for _, d in DIL_PATTERNS)
    return pl.pallas_call(
        _merge_kernel,
        grid=(n_tok // tm,),
        in_specs=[row(D_MODEL), pl.BlockSpec((MLA_HEADS, 1, V_HEAD_DIM, tm), lambda i: (0, i, 0, 0)),
                  *o_specs, *lse_specs,
                  _resident((1, D_MODEL)), _resident((D_MODEL, 2 * D_MODEL)), _resident((1, 2 * D_MODEL)),
                  _resident((MLA_OUT, D_MODEL)), _resident((DIL_OUT, D_MODEL)), _resident((D_MODEL, D_MODEL))],
        out_specs=row(D_MODEL),
        out_shape=jax.ShapeDtypeStruct((n_tok, D_MODEL), _F32),
        scratch_shapes=[pltpu.VMEM((tm, DIL_HEAD_DIM), _F32)] * n_strided
        + [pltpu.VMEM((DIL_HEADS_PER_GROUP, tm, DIL_HEAD_DIM), _F32)] * n_strided,
        compiler_params=_params("parallel"),
        name="merge",
    )(x, omla_t, *dil_o, *dil_lse, lw["mix_norm"], lw["w_gate"], lw["b_gate"], lw["w_branch_mla"], lw["w_branch_dil"], lw["w_out"])


def _prep_layer(l, ffn1_norm, ffn1_w_gate, ffn1_w_up, ffn1_w_down, mix_norm, w_in, b_gate, q_a_norm, w_q_up,
                kv_a_norm, w_kv_up, w_branch_mla, w_branch_dil, w_out, ffn2_norm, ffn2_w_gate, ffn2_w_up,
                ffn2_w_down):
    half = QK_ROPE_DIM // 2
    w = w_in[l]
    zeros = lambda r, c: jnp.zeros((r, c), _F32)

    k_pe = w[:, Q_LORA_RANK + KV_LORA_RANK:MLA_IN]
    pad_lo, pad_hi = QK_NOPE_DIM, MLA_HEAD_PAD - QK_NOPE_DIM - QK_ROPE_DIM
    k_pe_cols = jnp.concatenate([zeros(D_MODEL, pad_lo), k_pe, zeros(D_MODEL, pad_hi)], axis=1)
    k_rot_cols = jnp.concatenate([zeros(D_MODEL, pad_lo), -k_pe[:, half:], k_pe[:, :half], zeros(D_MODEL, pad_hi)], axis=1)
    w_lat = jnp.concatenate([w[:, :Q_LORA_RANK + KV_LORA_RANK], k_pe_cols, k_rot_cols], axis=1)

    wq = w_q_up[l].reshape(Q_LORA_RANK, MLA_HEADS, QK_NOPE_DIM + QK_ROPE_DIM)
    zq = jnp.zeros((Q_LORA_RANK, MLA_HEADS, pad_hi), _F32)
    wq_pad = jnp.concatenate([wq, zq], axis=2).reshape(Q_LORA_RANK, -1)
    q1, q2 = wq[:, :, QK_NOPE_DIM:QK_NOPE_DIM + half], wq[:, :, QK_NOPE_DIM + half:]
    wq_rot = jnp.concatenate([jnp.zeros((Q_LORA_RANK, MLA_HEADS, QK_NOPE_DIM), _F32), -q2, q1, zq],
                             axis=2).reshape(Q_LORA_RANK, -1)

    wkv = w_kv_up[l].reshape(KV_LORA_RANK, MLA_HEADS, QK_NOPE_DIM + V_HEAD_DIM)
    wk = jnp.concatenate([wkv[:, :, :QK_NOPE_DIM],
                          jnp.zeros((KV_LORA_RANK, MLA_HEADS, MLA_HEAD_PAD - QK_NOPE_DIM), _F32)],
                         axis=2).reshape(KV_LORA_RANK, -1)
    wvt = wkv[:, :, QK_NOPE_DIM:].reshape(KV_LORA_RANK, MLA_OUT).T

    bf = lambda t: t.astype(_BF16)
    vec = lambda t: t.reshape(1, -1)
    return dict(
        ffn1=(vec(ffn1_norm[l]), bf(ffn1_w_gate[l]), bf(ffn1_w_up[l]), bf(ffn1_w_down[l])),
        ffn2=(vec(ffn2_norm[l]), bf(ffn2_w_gate[l]), bf(ffn2_w_up[l]), bf(ffn2_w_down[l])),
        mix_norm=vec(mix_norm[l]), w_lat=bf(w_lat), w_dil=bf(w[:, MLA_IN:MLA_IN + 3 * DIL_QKV]),
        w_gate=bf(w[:, MLA_IN + 3 * DIL_QKV:]), b_gate=vec(b_gate[l]),
        q_a_norm=vec(q_a_norm[l]), kv_a_norm=vec(kv_a_norm[l]),
        wq=bf(jnp.concatenate([wq_pad, wq_rot], axis=1).T), wk=bf(wk), wvt=bf(wvt),
        w_branch_mla=bf(w_branch_mla[l]), w_branch_dil=bf(w_branch_dil[l]), w_out=bf(w_out[l]),
    )


def _rope_tables(seq):
    pos = jnp.arange(seq, dtype=_F32)
    inv_freq = 1.0 / (ROPE_THETA ** (jnp.arange(0, QK_ROPE_DIM, 2, dtype=_F32) / QK_ROPE_DIM))
    ang = pos[:, None] * inv_freq[None, :]
    cos, sin = jnp.cos(ang), jnp.sin(ang)
    pad_hi = MLA_HEAD_PAD - QK_NOPE_DIM - QK_ROPE_DIM
    cos_t = jnp.concatenate([jnp.ones((seq, QK_NOPE_DIM), _F32), cos, cos, jnp.ones((seq, pad_hi), _F32)], axis=1)
    sin_t = jnp.concatenate([jnp.zeros((seq, QK_NOPE_DIM), _F32), sin, sin, jnp.zeros((seq, pad_hi), _F32)], axis=1)
    return cos_t, sin_t


def _alibi_slopes(n):
    return 2.0 ** (-8.0 * jnp.arange(1, n + 1, dtype=_F32) / n)


def _trunk(x, layers, final_g):
    n_seq, seq, _ = x.shape
    x = x.reshape(n_seq * seq, D_MODEL)
    cos_t, sin_t = _rope_tables(seq)
    rope = (cos_t, sin_t, cos_t.T, sin_t.T)
    slopes = _alibi_slopes(DIL_HEADS)
    biases = [_dilated_bias(slopes, g, seq // d) for g, (_, d) in enumerate(DIL_PATTERNS)]
    for l, lw in enumerate(layers):
        x = _ffn(x, *lw["ffn1"], final_g, final_norm=False)
        qt, k, vt, dil_qkv = _proj(x, n_seq, seq, lw, rope)
        omla_t = _mla(qt, k, vt, n_seq, seq)
        dil = [_dilated(biases[g], *dil_qkv[g], g) for g in range(len(DIL_PATTERNS))]
        x = _merge(x, seq, omla_t, [o for o, _ in dil], [lse for _, lse in dil], lw)
        x = _ffn(x, *lw["ffn2"], final_g, final_norm=(l == len(layers) - 1))
    return x.reshape(n_seq, seq, D_MODEL)


def kernel(x_prompt, x_sample, ffn1_norm, ffn1_w_gate, ffn1_w_up, ffn1_w_down, mix_norm, w_in, b_gate, q_a_norm,
           w_q_up, kv_a_norm, w_kv_up, w_branch_mla, w_branch_dil, w_out, ffn2_norm, ffn2_w_gate, ffn2_w_up,
           ffn2_w_down, final_norm):
    stacked = (ffn1_norm, ffn1_w_gate, ffn1_w_up, ffn1_w_down, mix_norm, w_in, b_gate, q_a_norm, w_q_up, kv_a_norm,
               w_kv_up, w_branch_mla, w_branch_dil, w_out, ffn2_norm, ffn2_w_gate, ffn2_w_up, ffn2_w_down)
    layers = [_prep_layer(l, *stacked) for l in range(ffn1_norm.shape[0])]
    final_g = final_norm.reshape(1, -1)
    return (_trunk(x_prompt, layers, final_g), _trunk(x_sample, layers, final_g))
```

```python
import functools
import math

import jax
import jax.numpy as jnp
from jax import lax
from jax.experimental import pallas as pl
from jax.experimental.pallas import tpu as pltpu

D_MODEL = 1024

MLA_HEADS = 8
Q_LORA_RANK = 256
KV_LORA_RANK = 128
QK_NOPE_DIM = 64
QK_ROPE_DIM = 32
V_HEAD_DIM = 64
ROPE_THETA = 10000.0
MLA_HEAD_PAD = 128
MLA_OUT = MLA_HEADS * V_HEAD_DIM
MLA_VT_ROWS = 80
MLA_Q_SCALE = (QK_NOPE_DIM + QK_ROPE_DIM) ** -0.5 * math.log2(math.e)

DIL_PATTERNS = ((128, 1), (512, 4), (2048, 16))
DIL_HEADS_PER_GROUP = 4
DIL_HEADS = DIL_HEADS_PER_GROUP * len(DIL_PATTERNS)
DIL_HEAD_DIM = 128
DIL_OUT = DIL_HEADS_PER_GROUP * DIL_HEAD_DIM
DIL_QKV = DIL_HEADS * DIL_HEAD_DIM

D_FF = 2816
RMS_EPS = 1e-6
NEG_INF = -1e30

MLA_IN = Q_LORA_RANK + KV_LORA_RANK + QK_ROPE_DIM

V7X_VMEM_LIMIT_BYTES = 56 * 1024 * 1024

TOKEN_TILE = 512
FFN_TOKEN_TILE = 1024
FFN_SLAB = 256
MLA_Q_TILE = 512
DIL_Q_BLOCK = 128
DIL_MAX_BLOCK_ROWS = 4096
DIL_LSE_LANES = DIL_HEAD_DIM // DIL_HEADS_PER_GROUP
DIL_CHAINS = 16
DIL_SCALE = DIL_HEAD_DIM ** -0.5
DEINTERLEAVE_STRIDE = 4
MLA_CHUNKS_PER_TRIP = 8

_BF16 = jnp.bfloat16
_F32 = jnp.float32
_NT = (((1,), (1,)), ((), ()))
_TN = (((0,), (0,)), ((), ()))


def _params(*semantics):
    return pltpu.CompilerParams(dimension_semantics=semantics, vmem_limit_bytes=V7X_VMEM_LIMIT_BYTES)


def _resident(shape):
    return pl.BlockSpec(shape, lambda *_: (0,) * len(shape), pipeline_mode=pl.Buffered(1))


def _rms(xf, g):
    return xf * lax.rsqrt(jnp.mean(xf * xf, axis=-1, keepdims=True) + RMS_EPS) * g


def _dot(a, b):
    return jnp.dot(a, b, preferred_element_type=_F32)


def _ffn_kernel(x_ref, g_ref, wg_ref, wu_ref, wd_ref, fg_ref, o_ref, *, final_norm):
    x = x_ref[...]
    h = _rms(x, g_ref[...]).astype(_BF16)
    down = jnp.zeros_like(x)
    for c0 in range(0, D_FF, FFN_SLAB):
        gate = _dot(h, wg_ref[:, c0:c0 + FFN_SLAB])
        up = _dot(h, wu_ref[:, c0:c0 + FFN_SLAB])
        act = (gate * jax.nn.sigmoid(gate) * up).astype(_BF16)
        down = down + _dot(act, wd_ref[c0:c0 + FFN_SLAB, :])
    y = x + 0.5 * down
    if final_norm:
        y = _rms(y, fg_ref[...])
    o_ref[...] = y


def _ffn(x, norm_g, wg, wu, wd, final_g, *, final_norm):
    n_tok = x.shape[0]
    tm = FFN_TOKEN_TILE
    row = pl.BlockSpec((tm, D_MODEL), lambda i: (i, 0))
    return pl.pallas_call(
        functools.partial(_ffn_kernel, final_norm=final_norm),
        grid=(n_tok // tm,),
        in_specs=[row, _resident((1, D_MODEL)), _resident((D_MODEL, D_FF)), _resident((D_MODEL, D_FF)),
                  _resident((D_FF, D_MODEL)), _resident((1, D_MODEL))],
        out_specs=row,
        out_shape=jax.ShapeDtypeStruct((n_tok, D_MODEL), _F32),
        compiler_params=_params("parallel"),
        name="ffn",
    )(x, norm_g, wg, wu, wd, final_g)


def _proj_kernel(x_ref, g_ref, cos_ref, sin_ref, cos_t_ref, sin_t_ref, w_lat_ref, w_dil_ref, qn_ref, kvn_ref,
                 wq_ref, wk_ref, wvt_ref, *refs):
    qt_ref, k_ref, vt_ref = refs[:3]
    n_dil = 3 * len(DIL_PATTERNS)
    dil_refs = refs[3:3 + n_dil]
    stage_refs = refs[3 + n_dil:-1]
    restage_ref = refs[-1]
    tm = x_ref.shape[0]

    h = _rms(x_ref[...], g_ref[...]).astype(_BF16)
    cos = cos_ref[...]
    sin = sin_ref[...]

    lat = _dot(h, w_lat_ref[...])
    y_dil = [_dot(h, w_dil_ref[:, j * DIL_QKV:(j + 1) * DIL_QKV]) for j in range(3)]
    for j in range(3):
        for g, (_, dilation) in enumerate(DIL_PATTERNS):
            y = y_dil[j][:, g * DIL_OUT:(g + 1) * DIL_OUT]
            out = dil_refs[j * len(DIL_PATTERNS) + g]
            if dilation == 1:
                out[0, 0] = y.astype(_BF16)
            else:
                stage = stage_refs[g - 1]
                for c in range(DIL_HEADS_PER_GROUP):
                    cs = slice(c * DIL_HEAD_DIM, (c + 1) * DIL_HEAD_DIM)
                    stage[c] = y[:, cs]
                    if dilation <= DEINTERLEAVE_STRIDE:
                        for r in range(dilation):
                            out[0, r, :, cs] = stage[c, pl.ds(r, tm // dilation, stride=dilation), :].astype(_BF16)
                    else:
                        quarter, inner = tm // DEINTERLEAVE_STRIDE, dilation // DEINTERLEAVE_STRIDE
                        for r4 in range(DEINTERLEAVE_STRIDE):
                            restage_ref[c, r4 * quarter:(r4 + 1) * quarter, :] = \
                                stage[c, pl.ds(r4, quarter, stride=DEINTERLEAVE_STRIDE), :]
                        for r in range(dilation):
                            r4, r_in = r % DEINTERLEAVE_STRIDE, r // DEINTERLEAVE_STRIDE
                            out[0, r, :, cs] = restage_ref[
                                c, pl.ds(r4 * quarter + r_in, tm // dilation, stride=inner), :].astype(_BF16)

    c_q = _rms(lat[:, :Q_LORA_RANK], qn_ref[...]).astype(_BF16)
    c_kv = _rms(lat[:, Q_LORA_RANK:Q_LORA_RANK + KV_LORA_RANK], kvn_ref[...]).astype(_BF16)
    o = Q_LORA_RANK + KV_LORA_RANK
    k_pe = lat[:, o:o + MLA_HEAD_PAD] * cos + lat[:, o + MLA_HEAD_PAD:o + 2 * MLA_HEAD_PAD] * sin

    q_t = lax.dot_general(wq_ref[...], c_q, _NT, preferred_element_type=_F32) * MLA_Q_SCALE
    cos_c, sin_c = cos_t_ref[...], sin_t_ref[...]
    half = QK_ROPE_DIM // 2
    k_nope = _dot(c_kv, wk_ref[...])
    v_t = lax.dot_general(wvt_ref[...], c_kv, _NT, preferred_element_type=_F32).astype(_BF16)
    ones = jnp.ones((MLA_VT_ROWS - V_HEAD_DIM, tm), _BF16)
    for hd in range(MLA_HEADS):
        hs = slice(hd * MLA_HEAD_PAD, (hd + 1) * MLA_HEAD_PAD)
        r0 = hd * MLA_HEAD_PAD + QK_NOPE_DIM
        x1, x2 = q_t[r0:r0 + half], q_t[r0 + half:r0 + 2 * half]
        qt_ref[0, hs, :] = q_t[hs].astype(_BF16)
        qt_ref[0, r0:r0 + half, :] = (x1 * cos_c - x2 * sin_c).astype(_BF16)
        qt_ref[0, r0 + half:r0 + 2 * half, :] = (x2 * cos_c + x1 * sin_c).astype(_BF16)
        k_ref[:, hs] = (k_nope[:, hs] + k_pe).astype(_BF16)
        r0 = hd * MLA_VT_ROWS
        vt_ref[0, r0:r0 + V_HEAD_DIM, :] = v_t[hd * V_HEAD_DIM:(hd + 1) * V_HEAD_DIM]
        vt_ref[0, r0 + V_HEAD_DIM:r0 + MLA_VT_ROWS, :] = ones


def _residue_spec(tm, tiles_per_seq, dilation, width):
    return pl.BlockSpec((1, dilation, tm // dilation, width),
                        lambda i: (i // tiles_per_seq, 0, i % tiles_per_seq, 0))


def _proj(x, n_seq, seq, lw, rope):
    n_tok = x.shape[0]
    tm = TOKEN_TILE
    tiles_per_seq = seq // tm
    row = lambda w: pl.BlockSpec((tm, w), lambda i: (i, 0))
    cos_r, sin_r, cos_c, sin_c = rope
    rope_rows = pl.BlockSpec((tm, MLA_HEAD_PAD), lambda i: (i % tiles_per_seq, 0))
    rope_cols = pl.BlockSpec((QK_ROPE_DIM // 2, tm), lambda i: (0, i % tiles_per_seq))
    n_lat = lw["w_lat"].shape[1]
    mla_w = MLA_HEADS * MLA_HEAD_PAD
    vt_rows = MLA_HEADS * MLA_VT_ROWS
    dil_shapes = [jax.ShapeDtypeStruct((n_seq, d, seq // d, DIL_OUT), _BF16) for _ in range(3) for _, d in DIL_PATTERNS]
    dil_specs = [_residue_spec(tm, tiles_per_seq, d, DIL_OUT) for _ in range(3) for _, d in DIL_PATTERNS]
    out_shape = [jax.ShapeDtypeStruct((n_tok // tm, mla_w, tm), _BF16), jax.ShapeDtypeStruct((n_tok, mla_w), _BF16),
                 jax.ShapeDtypeStruct((n_tok // tm, vt_rows, tm), _BF16), *dil_shapes]
    out_specs = [pl.BlockSpec((1, mla_w, tm), lambda i: (i, 0, 0)), row(mla_w),
                 pl.BlockSpec((1, vt_rows, tm), lambda i: (i, 0, 0)), *dil_specs]
    outs = pl.pallas_call(
        _proj_kernel,
        grid=(n_tok // tm,),
        in_specs=[row(D_MODEL), _resident((1, D_MODEL)), rope_rows, rope_rows, rope_cols, rope_cols,
                  _resident((D_MODEL, n_lat)), _resident((D_MODEL, 3 * DIL_QKV)),
                  _resident((1, Q_LORA_RANK)), _resident((1, KV_LORA_RANK)),
                  _resident((mla_w, Q_LORA_RANK)),
                  _resident((KV_LORA_RANK, mla_w)), _resident((MLA_OUT, KV_LORA_RANK))],
        out_specs=out_specs,
        out_shape=out_shape,
        scratch_shapes=[pltpu.VMEM((DIL_HEADS_PER_GROUP, tm, DIL_HEAD_DIM), _F32) for _, d in DIL_PATTERNS if d > 1]
        + [pltpu.VMEM((DIL_HEADS_PER_GROUP, tm, DIL_HEAD_DIM), _F32)],
        compiler_params=_params("parallel"),
        name="proj",
    )(x, lw["mix_norm"], cos_r, sin_r, cos_c, sin_c, lw["w_lat"], lw["w_dil"], lw["q_a_norm"],
      lw["kv_a_norm"], lw["wq"], lw["wk"], lw["wvt"])
    q, k, vt = outs[:3]
    n_g = len(DIL_PATTERNS)
    dil_qkv = [tuple(outs[3 + j * n_g + g] for j in range(3)) for g in range(n_g)]
    return q, k, vt, dil_qkv


def _mla_kernel(qt_ref, k_ref, vt_ref, o_ref, s_ref, *, n_q, n_chunks, chunk, tq, per_trip):
    total = n_q * n_chunks

    def scores(g, slot):
        k0 = pl.multiple_of((g % n_chunks) * chunk, chunk)
        s_t = _dot(k_ref[pl.ds(k0, chunk), :], qt_ref[g // n_chunks])
        s_ref[slot] = s_t
        return jnp.max(s_t, axis=0, keepdims=True)

    def update(c, slot, m, m_chunk, acc):
        m_new = jnp.maximum(m, m_chunk)
        p = jnp.exp2(s_ref[slot] - m_new).astype(_BF16)
        return m_new, acc * jnp.exp2(m - m_new) + _dot(vt_ref[c], p)

    def trip(i, m_chunk):
        g0 = i * per_trip
        for j in range(per_trip):
            c = j % n_chunks
            if c == 0:
                m = jnp.full((1, tq), NEG_INF, _F32)
                acc = jnp.zeros((MLA_VT_ROWS, tq), _F32)
            m_next = scores(jnp.minimum(g0 + j + 1, total - 1), (j + 1) % per_trip)
            m, acc = update(c, j, m, m_chunk, acc)
            m_chunk = m_next
            if c == n_chunks - 1:
                o_ref[0, (g0 + j) // n_chunks] = (acc[:V_HEAD_DIM] / acc[V_HEAD_DIM:V_HEAD_DIM + 1]).astype(_BF16)
        return m_chunk

    lax.fori_loop(0, total // per_trip, trip, scores(0, 0))


def _mla(qt, k, vt, n_seq, seq):
    n_tok = k.shape[0]
    tq = qt.shape[2]
    chunk = vt.shape[2]
    n_q = seq // tq
    n_chunks = seq // chunk
    per_trip = MLA_CHUNKS_PER_TRIP
    assert per_trip % n_chunks == 0 and (n_q * n_chunks) % per_trip == 0
    kern = functools.partial(_mla_kernel, n_q=n_q, n_chunks=n_chunks, chunk=chunk, tq=tq, per_trip=per_trip)
    return pl.pallas_call(
        kern,
        grid=(n_seq, MLA_HEADS),
        in_specs=[pl.BlockSpec((n_q, MLA_HEAD_PAD, tq), lambda b, h: (b, h, 0)),
                  pl.BlockSpec((seq, MLA_HEAD_PAD), lambda b, h: (b, h)),
                  pl.BlockSpec((n_chunks, MLA_VT_ROWS, chunk), lambda b, h: (b, h, 0))],
        out_specs=pl.BlockSpec((1, n_q, V_HEAD_DIM, tq), lambda b, h: (h, b, 0, 0)),
        out_shape=jax.ShapeDtypeStruct((MLA_HEADS, n_tok // tq, V_HEAD_DIM, tq), _BF16),
        scratch_shapes=[pltpu.VMEM((per_trip, chunk, tq), _F32)],
        compiler_params=_params("parallel", "parallel"),
        name="mla_attention",
    )(qt, k, vt)


def _dilated_geometry(seg, half):
    bq = min(DIL_Q_BLOCK, seg)
    kw = min(bq + 2 * half, seg)
    return bq, kw, seg // bq


def _dilated_bias(slopes, group, seg):
    window, dilation = DIL_PATTERNS[group]
    half = window // (2 * dilation)
    bq, kw, _ = _dilated_geometry(seg, half)
    rel = jnp.arange(kw)[None, :] - jnp.arange(bq)[:, None]
    hs = slice(group * DIL_HEADS_PER_GROUP, (group + 1) * DIL_HEADS_PER_GROUP)
    tables = []
    for off in (0, -half, bq - kw):
        dist = jnp.abs(rel + off)
        alibi = -slopes[hs, None, None] * (dilation * dist).astype(_F32)[None]
        tables.append(jnp.where((dist <= half)[None], alibi / DIL_SCALE, NEG_INF))
    return jnp.stack(tables)


def _dilated_kernel(bias_ref, q_ref, k_ref, v_ref, o_ref, lse_ref, *, half, seg, residues):
    bq, kw, n_blk = _dilated_geometry(seg, half)
    exp2_scale = DIL_SCALE * math.log2(math.e)
    heads = DIL_HEADS_PER_GROUP
    n_units = residues * n_blk
    units_per_trip = min(n_units, max(1, DIL_CHAINS // heads))
    lane_head = lax.broadcasted_iota(jnp.int32, (bq, DIL_HEAD_DIM), 1) // DIL_LSE_LANES

    def body(t, carry):
        units = []
        for j in range(units_per_trip):
            u = t * units_per_trip + j
            r = u // n_blk
            i = u % n_blk
            q0 = pl.multiple_of(i * bq, bq)
            ks = pl.multiple_of(jnp.clip(q0 - half, 0, seg - kw), half)
            variant = jnp.where(i == 0, 0, jnp.where(i == n_blk - 1, 2, 1))
            units.append((r, q0, ks, variant))
        chains = [(r, q0, ks, variant, hd, slice(hd * DIL_HEAD_DIM, (hd + 1) * DIL_HEAD_DIM))
                  for r, q0, ks, variant in units for hd in range(heads)]
        x = jnp.concatenate(
            [lax.dot_general(q_ref[0, r, pl.ds(q0, bq), hs], k_ref[0, r, pl.ds(ks, kw), hs], _NT,
                             preferred_element_type=_F32) + bias_ref[variant, hd]
             for r, q0, ks, variant, hd, hs in chains], axis=0)
        m = jnp.max(x, axis=-1, keepdims=True)
        p = jnp.exp2((x - m) * exp2_scale)
        l = jnp.sum(p, axis=-1, keepdims=True)
        p = p.astype(_BF16)
        lse = m * DIL_SCALE + jnp.log(l)
        for c, (r, q0, ks, variant, hd, hs) in enumerate(chains):
            rows = slice(c * bq, (c + 1) * bq)
            o = _dot(p[rows], v_ref[0, r, pl.ds(ks, kw), hs]) / l[rows]
            o_ref[0, r, pl.ds(q0, bq), hs] = o.astype(_BF16)
        for j, (r, q0, ks, variant) in enumerate(units):
            tile = lse[(j * heads) * bq:(j * heads + 1) * bq]
            for hd in range(1, heads):
                tile = jnp.where(lane_head == hd, lse[(j * heads + hd) * bq:(j * heads + hd + 1) * bq], tile)
            lse_ref[0, r, pl.ds(q0, bq), :] = tile
        return carry

    lax.fori_loop(0, n_units // units_per_trip, body, 0)


def _dilated(bias, dq, dk, dv, group):
    window, dilation = DIL_PATTERNS[group]
    n_seq, _, seg, _ = dq.shape
    half = window // (2 * dilation)
    residues = max(1, min(dilation, DIL_MAX_BLOCK_ROWS // seg))
    spec = lambda w: pl.BlockSpec((1, residues, seg, w), lambda b, r: (b, r, 0, 0))
    kern = functools.partial(_dilated_kernel, half=half, seg=seg, residues=residues)
    return pl.pallas_call(
        kern,
        grid=(n_seq, dilation // residues),
        in_specs=[_resident(bias.shape), spec(DIL_OUT), spec(DIL_OUT), spec(DIL_OUT)],
        out_specs=(spec(DIL_OUT), spec(DIL_HEAD_DIM)),
        out_shape=(jax.ShapeDtypeStruct(dq.shape, _BF16),
                   jax.ShapeDtypeStruct(dq.shape[:3] + (DIL_HEAD_DIM,), _F32)),
        compiler_params=_params("parallel", "parallel"),
        name=f"dilated_g{group}",
    )(bias, dq, dk, dv)


def _interleave_rows(stage, restage, residue_rows, tm, dilation):
    if dilation <= DEINTERLEAVE_STRIDE:
        for r in range(dilation):
            stage[pl.ds(r, tm // dilation, stride=dilation), :] = residue_rows(r)
        return
    quarter, inner = tm // DEINTERLEAVE_STRIDE, dilation // DEINTERLEAVE_STRIDE
    for r in range(dilation):
        r4, r_in = r % DEINTERLEAVE_STRIDE, r // DEINTERLEAVE_STRIDE
        restage[pl.ds(r4 * quarter + r_in, tm // dilation, stride=inner), :] = residue_rows(r)
    for r4 in range(DEINTERLEAVE_STRIDE):
        stage[pl.ds(r4, quarter, stride=DEINTERLEAVE_STRIDE), :] = restage[r4 * quarter:(r4 + 1) * quarter, :]


def _merge_kernel(x_ref, omla_t_ref, o0_ref, o1_ref, o2_ref, l0_ref, l1_ref, l2_ref, g_ref, w_gate_ref, b_gate_ref,
                  wm_ref, wd_ref, wo_ref, out_ref, *stage_refs):
    tm = x_ref.shape[0]
    x = x_ref[...]
    gates = jax.nn.sigmoid(_dot(_rms(x, g_ref[...]).astype(_BF16), w_gate_ref[...]) + b_gate_ref[...])
    lse_stages, o_stages, restage_refs = stage_refs[:2], stage_refs[2:4], stage_refs[4:]

    def lse_token_order(ref, stage):
        dilation = ref.shape[1]
        if dilation == 1:
            return ref[0, 0]
        _interleave_rows(stage, restage_refs[0], lambda r: ref[0, r], tm, dilation)
        return stage[...]

    def o_token_order(ref, stage, hd):
        dilation = ref.shape[1]
        hs = slice(hd * DIL_HEAD_DIM, (hd + 1) * DIL_HEAD_DIM)
        if dilation == 1:
            return ref[0, 0, :, hs].astype(_F32)
        _interleave_rows(stage.at[hd], restage_refs[1], lambda r: ref[0, r, :, hs].astype(_F32), tm, dilation)
        return stage[hd]

    l0 = lse_token_order(l0_ref, None)
    l1 = lse_token_order(l1_ref, lse_stages[0])
    l2 = lse_token_order(l2_ref, lse_stages[1])
    m = jnp.maximum(jnp.maximum(l0, l1), l2)
    e0, e1, e2 = jnp.exp(l0 - m), jnp.exp(l1 - m), jnp.exp(l2 - m)
    den = e0 + e1 + e2
    w0, w1, w2 = e0 / den, e1 / den, e2 / den
    o_heads = []
    for hd in range(DIL_HEADS_PER_GROUP):
        lane = slice(hd * DIL_LSE_LANES, hd * DIL_LSE_LANES + 1)
        o_heads.append(w0[:, lane] * o_token_order(o0_ref, None, hd)
                       + w1[:, lane] * o_token_order(o1_ref, o_stages[0], hd)
                       + w2[:, lane] * o_token_order(o2_ref, o_stages[1], hd))
    o_dil = jnp.concatenate(o_heads, axis=1)
    omla_t = omla_t_ref[:, 0].reshape(MLA_OUT, tm)
    y_mla = lax.dot_general(omla_t, wm_ref[...], _TN, preferred_element_type=_F32)
    y_dil = _dot(o_dil.astype(_BF16), wd_ref[...])
    merged = gates[:, :D_MODEL] * y_mla + gates[:, D_MODEL:] * y_dil
    out_ref[...] = x + _dot(merged.astype(_BF16), wo_ref[...])


def _merge(x, seq, omla_t, dil_o, dil_lse, lw):
    n_tok = x.shape[0]
    tm = TOKEN_TILE
    tiles_per_seq = seq // tm
    row = lambda w: pl.BlockSpec((tm, w), lambda i: (i, 0))
    o_specs = [_residue_spec(tm, tiles_per_seq, d, DIL_OUT) for _, d in DIL_PATTERNS]
    lse_specs = [_residue_spec(tm, tiles_per_seq, d, DIL_HEAD_DIM) for _, d in DIL_PATTERNS]
    n_strided = sum(d > 1 for _, d in DIL_PATTERNS)
    return pl.pallas_call(
        _merge_kernel,
        grid=(n_tok // tm,),
        in_specs=[row(D_MODEL), pl.BlockSpec((MLA_HEADS, 1, V_HEAD_DIM, tm), lambda i: (0, i, 0, 0)),
                  *o_specs, *lse_specs,
                  _resident((1, D_MODEL)), _resident((D_MODEL, 2 * D_MODEL)), _resident((1, 2 * D_MODEL)),
                  _resident((MLA_OUT, D_MODEL)), _resident((DIL_OUT, D_MODEL)), _resident((D_MODEL, D_MODEL))],
        out_specs=row(D_MODEL),
        out_shape=jax.ShapeDtypeStruct((n_tok, D_MODEL), _F32),
        scratch_shapes=[pltpu.VMEM((tm, DIL_HEAD_DIM), _F32)] * n_strided
        + [pltpu.VMEM((DIL_HEADS_PER_GROUP, tm, DIL_HEAD_DIM), _F32)] * n_strided
        + [pltpu.VMEM((tm, DIL_HEAD_DIM), _F32)] * 2,
        compiler_params=_params("parallel"),
        name="merge",
    )(x, omla_t, *dil_o, *dil_lse, lw["mix_norm"], lw["w_gate"], lw["b_gate"], lw["w_branch_mla"], lw["w_branch_dil"], lw["w_out"])


def _prep_layer(l, ffn1_norm, ffn1_w_gate, ffn1_w_up, ffn1_w_down, mix_norm, w_in, b_gate, q_a_norm, w_q_up,
                kv_a_norm, w_kv_up, w_branch_mla, w_branch_dil, w_out, ffn2_norm, ffn2_w_gate, ffn2_w_up,
                ffn2_w_down):
    half = QK_ROPE_DIM // 2
    w = w_in[l]
    zeros = lambda r, c: jnp.zeros((r, c), _F32)

    k_pe = w[:, Q_LORA_RANK + KV_LORA_RANK:MLA_IN]
    pad_lo, pad_hi = QK_NOPE_DIM, MLA_HEAD_PAD - QK_NOPE_DIM - QK_ROPE_DIM
    k_pe_cols = jnp.concatenate([zeros(D_MODEL, pad_lo), k_pe, zeros(D_MODEL, pad_hi)], axis=1)
    k_rot_cols = jnp.concatenate([zeros(D_MODEL, pad_lo), -k_pe[:, half:], k_pe[:, :half], zeros(D_MODEL, pad_hi)], axis=1)
    w_lat = jnp.concatenate([w[:, :Q_LORA_RANK + KV_LORA_RANK], k_pe_cols, k_rot_cols], axis=1)

    wq = w_q_up[l].reshape(Q_LORA_RANK, MLA_HEADS, QK_NOPE_DIM + QK_ROPE_DIM)
    zq = jnp.zeros((Q_LORA_RANK, MLA_HEADS, pad_hi), _F32)
    wq_pad = jnp.concatenate([wq, zq], axis=2).reshape(Q_LORA_RANK, -1)
    wkv = w_kv_up[l].reshape(KV_LORA_RANK, MLA_HEADS, QK_NOPE_DIM + V_HEAD_DIM)
    wk = jnp.concatenate([wkv[:, :, :QK_NOPE_DIM],
                          jnp.zeros((KV_LORA_RANK, MLA_HEADS, MLA_HEAD_PAD - QK_NOPE_DIM), _F32)],
                         axis=2).reshape(KV_LORA_RANK, -1)
    wvt = wkv[:, :, QK_NOPE_DIM:].reshape(KV_LORA_RANK, MLA_OUT).T

    bf = lambda t: t.astype(_BF16)
    vec = lambda t: t.reshape(1, -1)
    return dict(
        ffn1=(vec(ffn1_norm[l]), bf(ffn1_w_gate[l]), bf(ffn1_w_up[l]), bf(ffn1_w_down[l])),
        ffn2=(vec(ffn2_norm[l]), bf(ffn2_w_gate[l]), bf(ffn2_w_up[l]), bf(ffn2_w_down[l])),
        mix_norm=vec(mix_norm[l]), w_lat=bf(w_lat), w_dil=bf(w[:, MLA_IN:MLA_IN + 3 * DIL_QKV]),
        w_gate=bf(w[:, MLA_IN + 3 * DIL_QKV:]), b_gate=vec(b_gate[l]),
        q_a_norm=vec(q_a_norm[l]), kv_a_norm=vec(kv_a_norm[l]),
        wq=bf(wq_pad.T), wk=bf(wk), wvt=bf(wvt),
        w_branch_mla=bf(w_branch_mla[l]), w_branch_dil=bf(w_branch_dil[l]), w_out=bf(w_out[l]),
    )


def _rope_tables(seq):
    pos = jnp.arange(seq, dtype=_F32)
    inv_freq = 1.0 / (ROPE_THETA ** (jnp.arange(0, QK_ROPE_DIM, 2, dtype=_F32) / QK_ROPE_DIM))
    ang = pos[:, None] * inv_freq[None, :]
    cos, sin = jnp.cos(ang), jnp.sin(ang)
    pad_hi = MLA_HEAD_PAD - QK_NOPE_DIM - QK_ROPE_DIM
    cos_t = jnp.concatenate([jnp.ones((seq, QK_NOPE_DIM), _F32), cos, cos, jnp.ones((seq, pad_hi), _F32)], axis=1)
    sin_t = jnp.concatenate([jnp.zeros((seq, QK_NOPE_DIM), _F32), sin, sin, jnp.zeros((seq, pad_hi), _F32)], axis=1)
    return cos_t, sin_t, cos.T, sin.T


def _alibi_slopes(n):
    return 2.0 ** (-8.0 * jnp.arange(1, n + 1, dtype=_F32) / n)


def _trunk(x, layers, final_g):
    n_seq, seq, _ = x.shape
    x = x.reshape(n_seq * seq, D_MODEL)
    rope = _rope_tables(seq)
    slopes = _alibi_slopes(DIL_HEADS)
    biases = [_dilated_bias(slopes, g, seq // d) for g, (_, d) in enumerate(DIL_PATTERNS)]
    for l, lw in enumerate(layers):
        x = _ffn(x, *lw["ffn1"], final_g, final_norm=False)
        qt, k, vt, dil_qkv = _proj(x, n_seq, seq, lw, rope)
        omla_t = _mla(qt, k, vt, n_seq, seq)
        dil = [_dilated(biases[g], *dil_qkv[g], g) for g in range(len(DIL_PATTERNS))]
        x = _merge(x, seq, omla_t, [o for o, _ in dil], [lse for _, lse in dil], lw)
        x = _ffn(x, *lw["ffn2"], final_g, final_norm=(l == len(layers) - 1))
    return x.reshape(n_seq, seq, D_MODEL)


def kernel(x_prompt, x_sample, ffn1_norm, ffn1_w_gate, ffn1_w_up, ffn1_w_down, mix_norm, w_in, b_gate, q_a_norm,
           w_q_up, kv_a_norm, w_kv_up, w_branch_mla, w_branch_dil, w_out, ffn2_norm, ffn2_w_gate, ffn2_w_up,
           ffn2_w_down, final_norm):
    stacked = (ffn1_norm, ffn1_w_gate, ffn1_w_up, ffn1_w_down, mix_norm, w_in, b_gate, q_a_norm, w_q_up, kv_a_norm,
               w_kv_up, w_branch_mla, w_branch_dil, w_out, ffn2_norm, ffn2_w_gate, ffn2_w_up, ffn2_w_down)
    layers = [_prep_layer(l, *stacked) for l in range(ffn1_norm.shape[0])]
    final_g = final_norm.reshape(1, -1)
    return (_trunk(x_prompt, layers, final_g), _trunk(x_sample, layers, final_g))
```

```python
import functools
import math

import jax
import jax.numpy as jnp
from jax import lax
from jax.experimental import pallas as pl
from jax.experimental.pallas import tpu as pltpu

D_MODEL = 1024

MLA_HEADS = 8
Q_LORA_RANK = 256
KV_LORA_RANK = 128
QK_NOPE_DIM = 64
QK_ROPE_DIM = 32
V_HEAD_DIM = 64
ROPE_THETA = 10000.0
MLA_HEAD_PAD = 128
MLA_OUT = MLA_HEADS * V_HEAD_DIM
MLA_VT_ROWS = 80
MLA_Q_SCALE = (QK_NOPE_DIM + QK_ROPE_DIM) ** -0.5 * math.log2(math.e)

DIL_PATTERNS = ((128, 1), (512, 4), (2048, 16))
DIL_HEADS_PER_GROUP = 4
DIL_HEADS = DIL_HEADS_PER_GROUP * len(DIL_PATTERNS)
DIL_HEAD_DIM = 128
DIL_OUT = DIL_HEADS_PER_GROUP * DIL_HEAD_DIM
DIL_QKV = DIL_HEADS * DIL_HEAD_DIM

D_FF = 2816
RMS_EPS = 1e-6
NEG_INF = -1e30

MLA_IN = Q_LORA_RANK + KV_LORA_RANK + QK_ROPE_DIM

V7X_VMEM_LIMIT_BYTES = 56 * 1024 * 1024

TOKEN_TILE = 512
FFN_TOKEN_TILE = 1024
FFN_SLAB = 256
DIL_Q_BLOCK = 128
DIL_MAX_BLOCK_ROWS = 4096
DIL_LSE_LANES = DIL_HEAD_DIM // DIL_HEADS_PER_GROUP
DIL_CHAINS = 16
DIL_SCALE = DIL_HEAD_DIM ** -0.5
DEINTERLEAVE_STRIDE = 4
MLA_CHUNKS_PER_TRIP = 8

_BF16 = jnp.bfloat16
_F32 = jnp.float32
_NT = (((1,), (1,)), ((), ()))
_TN = (((0,), (0,)), ((), ()))


def _params(*semantics):
    return pltpu.CompilerParams(dimension_semantics=semantics, vmem_limit_bytes=V7X_VMEM_LIMIT_BYTES)


def _resident(shape):
    return pl.BlockSpec(shape, lambda *_: (0,) * len(shape), pipeline_mode=pl.Buffered(1))


def _rms(xf, g):
    return xf * lax.rsqrt(jnp.mean(xf * xf, axis=-1, keepdims=True) + RMS_EPS) * g


def _dot(a, b):
    return jnp.dot(a, b, preferred_element_type=_F32)


def _ffn_kernel(x_ref, g_ref, wg_ref, wu_ref, wd_ref, fg_ref, o_ref, *, final_norm):
    x = x_ref[...]
    h = _rms(x, g_ref[...]).astype(_BF16)
    down = jnp.zeros_like(x)
    for c0 in range(0, D_FF, FFN_SLAB):
        gate = _dot(h, wg_ref[:, c0:c0 + FFN_SLAB])
        up = _dot(h, wu_ref[:, c0:c0 + FFN_SLAB])
        act = (gate * jax.nn.sigmoid(gate) * up).astype(_BF16)
        down = down + _dot(act, wd_ref[c0:c0 + FFN_SLAB, :])
    y = x + 0.5 * down
    if final_norm:
        y = _rms(y, fg_ref[...])
    o_ref[...] = y


def _ffn(x, norm_g, wg, wu, wd, final_g, *, final_norm):
    n_tok = x.shape[0]
    tm = FFN_TOKEN_TILE
    row = pl.BlockSpec((tm, D_MODEL), lambda i: (i, 0))
    return pl.pallas_call(
        functools.partial(_ffn_kernel, final_norm=final_norm),
        grid=(n_tok // tm,),
        in_specs=[row, _resident((1, D_MODEL)), _resident((D_MODEL, D_FF)), _resident((D_MODEL, D_FF)),
                  _resident((D_FF, D_MODEL)), _resident((1, D_MODEL))],
        out_specs=row,
        out_shape=jax.ShapeDtypeStruct((n_tok, D_MODEL), _F32),
        compiler_params=_params("parallel"),
        name="ffn",
    )(x, norm_g, wg, wu, wd, final_g)


def _proj_kernel(x_ref, g_ref, cos_ref, sin_ref, cos_t_ref, sin_t_ref, w_lat_ref, w_dil_ref, qn_ref,
                 kvn_ref, wq_ref, wk_ref, wvt_ref, *refs):
    qt_ref, k_ref, vt_ref = refs[:3]
    n_dil = 3 * len(DIL_PATTERNS)
    dil_refs = refs[3:3 + n_dil]
    stage_refs = refs[3 + n_dil:-1]
    restage_ref = refs[-1]
    tm = x_ref.shape[0]

    h = _rms(x_ref[...], g_ref[...]).astype(_BF16)
    cos = cos_ref[...]
    sin = sin_ref[...]

    lat = _dot(h, w_lat_ref[...])
    y_dil = [_dot(h, w_dil_ref[:, j * DIL_QKV:(j + 1) * DIL_QKV]) for j in range(3)]
    for j in range(3):
        for g, (_, dilation) in enumerate(DIL_PATTERNS):
            y = y_dil[j][:, g * DIL_OUT:(g + 1) * DIL_OUT]
            out = dil_refs[j * len(DIL_PATTERNS) + g]
            if dilation == 1:
                out[0, 0] = y.astype(_BF16)
            else:
                stage = stage_refs[g - 1]
                for c in range(DIL_HEADS_PER_GROUP):
                    cs = slice(c * DIL_HEAD_DIM, (c + 1) * DIL_HEAD_DIM)
                    stage[c] = y[:, cs]
                    if dilation <= DEINTERLEAVE_STRIDE:
                        for r in range(dilation):
                            out[0, r, :, cs] = stage[c, pl.ds(r, tm // dilation, stride=dilation), :].astype(_BF16)
                    else:
                        quarter, inner = tm // DEINTERLEAVE_STRIDE, dilation // DEINTERLEAVE_STRIDE
                        for r4 in range(DEINTERLEAVE_STRIDE):
                            restage_ref[c, r4 * quarter:(r4 + 1) * quarter, :] = \
                                stage[c, pl.ds(r4, quarter, stride=DEINTERLEAVE_STRIDE), :]
                        for r in range(dilation):
                            r4, r_in = r % DEINTERLEAVE_STRIDE, r // DEINTERLEAVE_STRIDE
                            out[0, r, :, cs] = restage_ref[
                                c, pl.ds(r4 * quarter + r_in, tm // dilation, stride=inner), :].astype(_BF16)

    c_q = _rms(lat[:, :Q_LORA_RANK], qn_ref[...]).astype(_BF16)
    c_kv = _rms(lat[:, Q_LORA_RANK:Q_LORA_RANK + KV_LORA_RANK], kvn_ref[...]).astype(_BF16)
    o = Q_LORA_RANK + KV_LORA_RANK
    k_pe = lat[:, o:o + MLA_HEAD_PAD] * cos + lat[:, o + MLA_HEAD_PAD:o + 2 * MLA_HEAD_PAD] * sin

    q_t = lax.dot_general(wq_ref[...], c_q, _NT, preferred_element_type=_F32) * MLA_Q_SCALE
    cos_c, sin_c = cos_t_ref[...], sin_t_ref[...]
    half = QK_ROPE_DIM // 2
    k_nope = _dot(c_kv, wk_ref[...])
    v_t = lax.dot_general(wvt_ref[...], c_kv, _NT, preferred_element_type=_F32).astype(_BF16)
    ones = jnp.ones((MLA_VT_ROWS - V_HEAD_DIM, tm), _BF16)
    for hd in range(MLA_HEADS):
        hs = slice(hd * MLA_HEAD_PAD, (hd + 1) * MLA_HEAD_PAD)
        r0 = hd * MLA_HEAD_PAD + QK_NOPE_DIM
        x1, x2 = q_t[r0:r0 + half], q_t[r0 + half:r0 + 2 * half]
        qt_ref[0, hs, :] = q_t[hs].astype(_BF16)
        qt_ref[0, r0:r0 + half, :] = (x1 * cos_c - x2 * sin_c).astype(_BF16)
        qt_ref[0, r0 + half:r0 + 2 * half, :] = (x2 * cos_c + x1 * sin_c).astype(_BF16)
        k_ref[:, hs] = (k_nope[:, hs] + k_pe).astype(_BF16)
        r0 = hd * MLA_VT_ROWS
        vt_ref[0, r0:r0 + V_HEAD_DIM, :] = v_t[hd * V_HEAD_DIM:(hd + 1) * V_HEAD_DIM]
        vt_ref[0, r0 + V_HEAD_DIM:r0 + MLA_VT_ROWS, :] = ones


def _residue_spec(tm, tiles_per_seq, dilation, width):
    return pl.BlockSpec((1, dilation, tm // dilation, width),
                        lambda i: (i // tiles_per_seq, 0, i % tiles_per_seq, 0))


def _proj(x, n_seq, seq, lw, rope):
    n_tok = x.shape[0]
    tm = TOKEN_TILE
    tiles_per_seq = seq // tm
    row = lambda w: pl.BlockSpec((tm, w), lambda i: (i, 0))
    cos_r, sin_r, cos_c, sin_c = rope
    rope_rows = pl.BlockSpec((tm, MLA_HEAD_PAD), lambda i: (i % tiles_per_seq, 0))
    rope_cols = pl.BlockSpec((QK_ROPE_DIM // 2, tm), lambda i: (0, i % tiles_per_seq))
    n_lat = lw["w_lat"].shape[1]
    mla_w = MLA_HEADS * MLA_HEAD_PAD
    vt_rows = MLA_HEADS * MLA_VT_ROWS
    dil_shapes = [jax.ShapeDtypeStruct((n_seq, d, seq // d, DIL_OUT), _BF16) for _ in range(3) for _, d in DIL_PATTERNS]
    dil_specs = [_residue_spec(tm, tiles_per_seq, d, DIL_OUT) for _ in range(3) for _, d in DIL_PATTERNS]
    out_shape = [jax.ShapeDtypeStruct((n_tok // tm, mla_w, tm), _BF16), jax.ShapeDtypeStruct((n_tok, mla_w), _BF16),
                 jax.ShapeDtypeStruct((n_tok // tm, vt_rows, tm), _BF16), *dil_shapes]
    out_specs = [pl.BlockSpec((1, mla_w, tm), lambda i: (i, 0, 0)), row(mla_w),
                 pl.BlockSpec((1, vt_rows, tm), lambda i: (i, 0, 0)), *dil_specs]
    outs = pl.pallas_call(
        _proj_kernel,
        grid=(n_tok // tm,),
        in_specs=[row(D_MODEL), _resident((1, D_MODEL)), rope_rows, rope_rows, rope_cols, rope_cols,
                  _resident((D_MODEL, n_lat)), _resident((D_MODEL, 3 * DIL_QKV)),
                  _resident((1, Q_LORA_RANK)), _resident((1, KV_LORA_RANK)),
                  _resident((mla_w, Q_LORA_RANK)),
                  _resident((KV_LORA_RANK, mla_w)), _resident((MLA_OUT, KV_LORA_RANK))],
        out_specs=out_specs,
        out_shape=out_shape,
        scratch_shapes=[pltpu.VMEM((DIL_HEADS_PER_GROUP, tm, DIL_HEAD_DIM), _F32) for _, d in DIL_PATTERNS if d > 1]
        + [pltpu.VMEM((DIL_HEADS_PER_GROUP, tm, DIL_HEAD_DIM), _F32)],
        compiler_params=_params("parallel"),
        name="proj",
    )(x, lw["mix_norm"], cos_r, sin_r, cos_c, sin_c, lw["w_lat"], lw["w_dil"], lw["q_a_norm"],
      lw["kv_a_norm"], lw["wq"], lw["wk"], lw["wvt"])
    q, k, vt = outs[:3]
    n_g = len(DIL_PATTERNS)
    dil_qkv = [tuple(outs[3 + j * n_g + g] for j in range(3)) for g in range(n_g)]
    return q, k, vt, dil_qkv


def _mla_kernel(qt_ref, k_ref, vt_ref, o_ref, s_ref, *, n_q, n_chunks, chunk, tq, per_trip):
    total = n_q * n_chunks

    def scores(g, slot):
        k0 = pl.multiple_of((g % n_chunks) * chunk, chunk)
        s_t = _dot(k_ref[pl.ds(k0, chunk), :], qt_ref[g // n_chunks])
        s_ref[slot] = s_t
        return jnp.max(s_t, axis=0, keepdims=True)

    def update(c, slot, m, m_chunk, acc):
        m_new = jnp.maximum(m, m_chunk)
        p = jnp.exp2(s_ref[slot] - m_new).astype(_BF16)
        return m_new, acc * jnp.exp2(m - m_new) + _dot(vt_ref[c], p)

    def trip(i, m_chunk):
        g0 = i * per_trip
        for j in range(per_trip):
            c = j % n_chunks
            if c == 0:
                m = jnp.full((1, tq), NEG_INF, _F32)
                acc = jnp.zeros((MLA_VT_ROWS, tq), _F32)
            m_next = scores(jnp.minimum(g0 + j + 1, total - 1), (j + 1) % per_trip)
            m, acc = update(c, j, m, m_chunk, acc)
            m_chunk = m_next
            if c == n_chunks - 1:
                o_ref[0, (g0 + j) // n_chunks] = (acc[:V_HEAD_DIM] / acc[V_HEAD_DIM:V_HEAD_DIM + 1]).astype(_BF16)
        return m_chunk

    lax.fori_loop(0, total // per_trip, trip, scores(0, 0))


def _mla(qt, k, vt, n_seq, seq):
    n_tok = k.shape[0]
    tq = qt.shape[2]
    chunk = vt.shape[2]
    n_q = seq // tq
    n_chunks = seq // chunk
    per_trip = MLA_CHUNKS_PER_TRIP
    assert per_trip % n_chunks == 0 and (n_q * n_chunks) % per_trip == 0
    kern = functools.partial(_mla_kernel, n_q=n_q, n_chunks=n_chunks, chunk=chunk, tq=tq, per_trip=per_trip)
    return pl.pallas_call(
        kern,
        grid=(n_seq, MLA_HEADS),
        in_specs=[pl.BlockSpec((n_q, MLA_HEAD_PAD, tq), lambda b, h: (b, h, 0)),
                  pl.BlockSpec((seq, MLA_HEAD_PAD), lambda b, h: (b, h)),
                  pl.BlockSpec((n_chunks, MLA_VT_ROWS, chunk), lambda b, h: (b, h, 0))],
        out_specs=pl.BlockSpec((1, n_q, V_HEAD_DIM, tq), lambda b, h: (h, b, 0, 0)),
        out_shape=jax.ShapeDtypeStruct((MLA_HEADS, n_tok // tq, V_HEAD_DIM, tq), _BF16),
        scratch_shapes=[pltpu.VMEM((per_trip, chunk, tq), _F32)],
        compiler_params=_params("parallel", "parallel"),
        name="mla_attention",
    )(qt, k, vt)


def _dilated_geometry(seg, half):
    bq = min(DIL_Q_BLOCK, seg)
    kw = min(bq + 2 * half, seg)
    return bq, kw, seg // bq


def _dilated_bias(slopes, group, seg):
    window, dilation = DIL_PATTERNS[group]
    half = window // (2 * dilation)
    bq, kw, _ = _dilated_geometry(seg, half)
    rel = jnp.arange(kw)[None, :] - jnp.arange(bq)[:, None]
    hs = slice(group * DIL_HEADS_PER_GROUP, (group + 1) * DIL_HEADS_PER_GROUP)
    tables = []
    for off in (0, -half, bq - kw):
        dist = jnp.abs(rel + off)
        alibi = -slopes[hs, None, None] * (dilation * dist).astype(_F32)[None]
        tables.append(jnp.where((dist <= half)[None], alibi / DIL_SCALE, NEG_INF))
    return jnp.stack(tables)


def _dilated_kernel(bias_ref, q_ref, k_ref, v_ref, o_ref, lse_ref, *, half, seg, residues):
    bq, kw, n_blk = _dilated_geometry(seg, half)
    exp2_scale = DIL_SCALE * math.log2(math.e)
    heads = DIL_HEADS_PER_GROUP
    n_units = residues * n_blk
    units_per_trip = min(n_units, max(1, DIL_CHAINS // heads))
    lane_head = lax.broadcasted_iota(jnp.int32, (bq, DIL_HEAD_DIM), 1) // DIL_LSE_LANES

    def body(t, carry):
        units = []
        for j in range(units_per_trip):
            u = t * units_per_trip + j
            r = u // n_blk
            i = u % n_blk
            q0 = pl.multiple_of(i * bq, bq)
            ks = pl.multiple_of(jnp.clip(q0 - half, 0, seg - kw), half)
            variant = jnp.where(i == 0, 0, jnp.where(i == n_blk - 1, 2, 1))
            units.append((r, q0, ks, variant))
        chains = [(r, q0, ks, variant, hd, slice(hd * DIL_HEAD_DIM, (hd + 1) * DIL_HEAD_DIM))
                  for r, q0, ks, variant in units for hd in range(heads)]
        x = jnp.concatenate(
            [lax.dot_general(q_ref[0, r, pl.ds(q0, bq), hs], k_ref[0, r, pl.ds(ks, kw), hs], _NT,
                             preferred_element_type=_F32) + bias_ref[variant, hd]
             for r, q0, ks, variant, hd, hs in chains], axis=0)
        m = jnp.max(x, axis=-1, keepdims=True)
        p = jnp.exp2((x - m) * exp2_scale)
        l = jnp.sum(p, axis=-1, keepdims=True)
        p = p.astype(_BF16)
        lse = m * DIL_SCALE + jnp.log(l)
        for c, (r, q0, ks, variant, hd, hs) in enumerate(chains):
            rows = slice(c * bq, (c + 1) * bq)
            o = _dot(p[rows], v_ref[0, r, pl.ds(ks, kw), hs]) / l[rows]
            o_ref[0, r, pl.ds(q0, bq), hs] = o.astype(_BF16)
        for j, (r, q0, ks, variant) in enumerate(units):
            tile = lse[(j * heads) * bq:(j * heads + 1) * bq]
            for hd in range(1, heads):
                tile = jnp.where(lane_head == hd, lse[(j * heads + hd) * bq:(j * heads + hd + 1) * bq], tile)
            lse_ref[0, r, pl.ds(q0, bq), :] = tile
        return carry

    lax.fori_loop(0, n_units // units_per_trip, body, 0)


def _dilated(bias, dq, dk, dv, group):
    window, dilation = DIL_PATTERNS[group]
    n_seq, _, seg, _ = dq.shape
    half = window // (2 * dilation)
    residues = max(1, min(dilation, DIL_MAX_BLOCK_ROWS // seg))
    spec = lambda w: pl.BlockSpec((1, residues, seg, w), lambda b, r: (b, r, 0, 0))
    kern = functools.partial(_dilated_kernel, half=half, seg=seg, residues=residues)
    return pl.pallas_call(
        kern,
        grid=(n_seq, dilation // residues),
        in_specs=[_resident(bias.shape), spec(DIL_OUT), spec(DIL_OUT), spec(DIL_OUT)],
        out_specs=(spec(DIL_OUT), spec(DIL_HEAD_DIM)),
        out_shape=(jax.ShapeDtypeStruct(dq.shape, _BF16),
                   jax.ShapeDtypeStruct(dq.shape[:3] + (DIL_HEAD_DIM,), _F32)),
        compiler_params=_params("parallel", "parallel"),
        name=f"dilated_g{group}",
    )(bias, dq, dk, dv)


def _interleave_rows(stage, restage, residue_rows, tm, dilation):
    if dilation <= DEINTERLEAVE_STRIDE:
        for r in range(dilation):
            stage[pl.ds(r, tm // dilation, stride=dilation), :] = residue_rows(r)
        return
    quarter, inner = tm // DEINTERLEAVE_STRIDE, dilation // DEINTERLEAVE_STRIDE
    for r in range(dilation):
        r4, r_in = r % DEINTERLEAVE_STRIDE, r // DEINTERLEAVE_STRIDE
        restage[pl.ds(r4 * quarter + r_in, tm // dilation, stride=inner), :] = residue_rows(r)
    for r4 in range(DEINTERLEAVE_STRIDE):
        stage[pl.ds(r4, quarter, stride=DEINTERLEAVE_STRIDE), :] = restage[r4 * quarter:(r4 + 1) * quarter, :]


def _merge_kernel(x_ref, omla_t_ref, o0_ref, o1_ref, o2_ref, l0_ref, l1_ref, l2_ref, g_ref, w_gate_ref, b_gate_ref,
                  wm_ref, wd_ref, wo_ref, out_ref, *stage_refs):
    tm = x_ref.shape[0]
    x = x_ref[...]
    gates = jax.nn.sigmoid(_dot(_rms(x, g_ref[...]).astype(_BF16), w_gate_ref[...]) + b_gate_ref[...])
    lse_stages, o_stages, restage_refs = stage_refs[:2], stage_refs[2:4], stage_refs[4:]

    def lse_token_order(ref, stage):
        dilation = ref.shape[1]
        if dilation == 1:
            return ref[0, 0]
        _interleave_rows(stage, restage_refs[0], lambda r: ref[0, r], tm, dilation)
        return stage[...]

    def o_token_order(ref, stage, hd):
        dilation = ref.shape[1]
        hs = slice(hd * DIL_HEAD_DIM, (hd + 1) * DIL_HEAD_DIM)
        if dilation == 1:
            return ref[0, 0, :, hs].astype(_F32)
        _interleave_rows(stage.at[hd], restage_refs[1], lambda r: ref[0, r, :, hs].astype(_F32), tm, dilation)
        return stage[hd]

    l0 = lse_token_order(l0_ref, None)
    l1 = lse_token_order(l1_ref, lse_stages[0])
    l2 = lse_token_order(l2_ref, lse_stages[1])
    m = jnp.maximum(jnp.maximum(l0, l1), l2)
    e0, e1, e2 = jnp.exp(l0 - m), jnp.exp(l1 - m), jnp.exp(l2 - m)
    den = e0 + e1 + e2
    w0, w1, w2 = e0 / den, e1 / den, e2 / den
    o_heads = []
    for hd in range(DIL_HEADS_PER_GROUP):
        lane = slice(hd * DIL_LSE_LANES, hd * DIL_LSE_LANES + 1)
        o_heads.append(w0[:, lane] * o_token_order(o0_ref, None, hd)
                       + w1[:, lane] * o_token_order(o1_ref, o_stages[0], hd)
                       + w2[:, lane] * o_token_order(o2_ref, o_stages[1], hd))
    o_dil = jnp.concatenate(o_heads, axis=1)
    omla_t = omla_t_ref[:, 0].reshape(MLA_OUT, tm)
    y_mla = lax.dot_general(omla_t, wm_ref[...], _TN, preferred_element_type=_F32)
    y_dil = _dot(o_dil.astype(_BF16), wd_ref[...])
    merged = gates[:, :D_MODEL] * y_mla + gates[:, D_MODEL:] * y_dil
    out_ref[...] = x + _dot(merged.astype(_BF16), wo_ref[...])


def _merge(x, seq, omla_t, dil_o, dil_lse, lw):
    n_tok = x.shape[0]
    tm = TOKEN_TILE
    tiles_per_seq = seq // tm
    row = lambda w: pl.BlockSpec((tm, w), lambda i: (i, 0))
    o_specs = [_residue_spec(tm, tiles_per_seq, d, DIL_OUT) for _, d in DIL_PATTERNS]
    lse_specs = [_residue_spec(tm, tiles_per_seq, d, DIL_HEAD_DIM) for _, d in DIL_PATTERNS]
    n_strided = sum(d > 1 for _, d in DIL_PATTERNS)
    return pl.pallas_call(
        _merge_kernel,
        grid=(n_tok // tm,),
        in_specs=[row(D_MODEL), pl.BlockSpec((MLA_HEADS, 1, V_HEAD_DIM, tm), lambda i: (0, i, 0, 0)),
                  *o_specs, *lse_specs,
                  _resident((1, D_MODEL)), _resident((D_MODEL, 2 * D_MODEL)), _resident((1, 2 * D_MODEL)),
                  _resident((MLA_OUT, D_MODEL)), _resident((DIL_OUT, D_MODEL)), _resident((D_MODEL, D_MODEL))],
        out_specs=row(D_MODEL),
        out_shape=jax.ShapeDtypeStruct((n_tok, D_MODEL), _F32),
        scratch_shapes=[pltpu.VMEM((tm, DIL_HEAD_DIM), _F32)] * n_strided
        + [pltpu.VMEM((DIL_HEADS_PER_GROUP, tm, DIL_HEAD_DIM), _F32)] * n_strided
        + [pltpu.VMEM((tm, DIL_HEAD_DIM), _F32)] * 2,
        compiler_params=_params("parallel"),
        name="merge",
    )(x, omla_t, *dil_o, *dil_lse, lw["mix_norm"], lw["w_gate"], lw["b_gate"], lw["w_branch_mla"],
      lw["w_branch_dil"], lw["w_out"])


def _prep_layer(l, ffn1_norm, ffn1_w_gate, ffn1_w_up, ffn1_w_down, mix_norm, w_in, b_gate, q_a_norm, w_q_up,
                kv_a_norm, w_kv_up, w_branch_mla, w_branch_dil, w_out, ffn2_norm, ffn2_w_gate, ffn2_w_up,
                ffn2_w_down):
    half = QK_ROPE_DIM // 2
    w = w_in[l]
    zeros = lambda r, c: jnp.zeros((r, c), _F32)

    k_pe = w[:, Q_LORA_RANK + KV_LORA_RANK:MLA_IN]
    pad_lo, pad_hi = QK_NOPE_DIM, MLA_HEAD_PAD - QK_NOPE_DIM - QK_ROPE_DIM
    k_pe_cols = jnp.concatenate([zeros(D_MODEL, pad_lo), k_pe, zeros(D_MODEL, pad_hi)], axis=1)
    k_rot_cols = jnp.concatenate([zeros(D_MODEL, pad_lo), -k_pe[:, half:], k_pe[:, :half], zeros(D_MODEL, pad_hi)],
                                 axis=1)
    w_lat = jnp.concatenate([w[:, :Q_LORA_RANK + KV_LORA_RANK], k_pe_cols, k_rot_cols], axis=1)

    wq = w_q_up[l].reshape(Q_LORA_RANK, MLA_HEADS, QK_NOPE_DIM + QK_ROPE_DIM)
    zq = jnp.zeros((Q_LORA_RANK, MLA_HEADS, pad_hi), _F32)
    wq_pad = jnp.concatenate([wq, zq], axis=2).reshape(Q_LORA_RANK, -1)
    wkv = w_kv_up[l].reshape(KV_LORA_RANK, MLA_HEADS, QK_NOPE_DIM + V_HEAD_DIM)
    wk = jnp.concatenate([wkv[:, :, :QK_NOPE_DIM],
                          jnp.zeros((KV_LORA_RANK, MLA_HEADS, MLA_HEAD_PAD - QK_NOPE_DIM), _F32)],
                         axis=2).reshape(KV_LORA_RANK, -1)
    wvt = wkv[:, :, QK_NOPE_DIM:].reshape(KV_LORA_RANK, MLA_OUT).T

    bf = lambda t: t.astype(_BF16)
    vec = lambda t: t.reshape(1, -1)
    return dict(
        ffn1=(vec(ffn1_norm[l]), bf(ffn1_w_gate[l]), bf(ffn1_w_up[l]), bf(ffn1_w_down[l])),
        ffn2=(vec(ffn2_norm[l]), bf(ffn2_w_gate[l]), bf(ffn2_w_up[l]), bf(ffn2_w_down[l])),
        mix_norm=vec(mix_norm[l]), w_lat=bf(w_lat), w_dil=bf(w[:, MLA_IN:MLA_IN + 3 * DIL_QKV]),
        w_gate=bf(w[:, MLA_IN + 3 * DIL_QKV:]), b_gate=vec(b_gate[l]),
        q_a_norm=vec(q_a_norm[l]), kv_a_norm=vec(kv_a_norm[l]),
        wq=bf(wq_pad.T), wk=bf(wk), wvt=bf(wvt),
        w_branch_mla=bf(w_branch_mla[l]), w_branch_dil=bf(w_branch_dil[l]), w_out=bf(w_out[l]),
    )


def _rope_tables(seq):
    pos = jnp.arange(seq, dtype=_F32)
    inv_freq = 1.0 / (ROPE_THETA ** (jnp.arange(0, QK_ROPE_DIM, 2, dtype=_F32) / QK_ROPE_DIM))
    ang = pos[:, None] * inv_freq[None, :]
    cos, sin = jnp.cos(ang), jnp.sin(ang)
    pad_hi = MLA_HEAD_PAD - QK_NOPE_DIM - QK_ROPE_DIM
    cos_t = jnp.concatenate([jnp.ones((seq, QK_NOPE_DIM), _F32), cos, cos, jnp.ones((seq, pad_hi), _F32)], axis=1)
    sin_t = jnp.concatenate([jnp.zeros((seq, QK_NOPE_DIM), _F32), sin, sin, jnp.zeros((seq, pad_hi), _F32)], axis=1)
    return cos_t, sin_t, cos.T, sin.T


def _alibi_slopes(n):
    return 2.0 ** (-8.0 * jnp.arange(1, n + 1, dtype=_F32) / n)


def _trunk(x, layers, final_g):
    n_seq, seq, _ = x.shape
    x = x.reshape(n_seq * seq, D_MODEL)
    rope = _rope_tables(seq)
    slopes = _alibi_slopes(DIL_HEADS)
    biases = [_dilated_bias(slopes, g, seq // d) for g, (_, d) in enumerate(DIL_PATTERNS)]
    for l, lw in enumerate(layers):
        x = _ffn(x, *lw["ffn1"], final_g, final_norm=False)
        qt, k, vt, dil_qkv = _proj(x, n_seq, seq, lw, rope)
        omla_t = _mla(qt, k, vt, n_seq, seq)
        dil = [_dilated(biases[g], *dil_qkv[g], g) for g in range(len(DIL_PATTERNS))]
        x = _merge(x, seq, omla_t, [o for o, _ in dil], [lse for _, lse in dil], lw)
        x = _ffn(x, *lw["ffn2"], final_g, final_norm=(l == len(layers) - 1))
    return x.reshape(n_seq, seq, D_MODEL)


def kernel(x_prompt, x_sample, ffn1_norm, ffn1_w_gate, ffn1_w_up, ffn1_w_down, mix_norm, w_in, b_gate, q_a_norm,
           w_q_up, kv_a_norm, w_kv_up, w_branch_mla, w_branch_dil, w_out, ffn2_norm, ffn2_w_gate, ffn2_w_up,
           ffn2_w_down, final_norm):
    stacked = (ffn1_norm, ffn1_w_gate, ffn1_w_up, ffn1_w_down, mix_norm, w_in, b_gate, q_a_norm, w_q_up, kv_a_norm,
               w_kv_up, w_branch_mla, w_branch_dil, w_out, ffn2_norm, ffn2_w_gate, ffn2_w_up, ffn2_w_down)
    layers = [_prep_layer(l, *stacked) for l in range(ffn1_norm.shape[0])]
    final_g = final_norm.reshape(1, -1)
    return (_trunk(x_prompt, layers, final_g), _trunk(x_sample, layers, final_g))
```

```python
import functools
import math

import jax
import jax.numpy as jnp
from jax import lax
from jax.experimental import pallas as pl
from jax.experimental.pallas import tpu as pltpu

D_MODEL = 1024

MLA_HEADS = 8
Q_LORA_RANK = 256
KV_LORA_RANK = 128
QK_NOPE_DIM = 64
QK_ROPE_DIM = 32
V_HEAD_DIM = 64
ROPE_THETA = 10000.0
MLA_HEAD_PAD = 128
MLA_OUT = MLA_HEADS * V_HEAD_DIM
MLA_VT_ROWS = 80
MLA_Q_SCALE = (QK_NOPE_DIM + QK_ROPE_DIM) ** -0.5 * math.log2(math.e)

DIL_PATTERNS = ((128, 1), (512, 4), (2048, 16))
DIL_HEADS_PER_GROUP = 4
DIL_HEADS = DIL_HEADS_PER_GROUP * len(DIL_PATTERNS)
DIL_HEAD_DIM = 128
DIL_OUT = DIL_HEADS_PER_GROUP * DIL_HEAD_DIM
DIL_QKV = DIL_HEADS * DIL_HEAD_DIM

D_FF = 2816
RMS_EPS = 1e-6
NEG_INF = -1e30

MLA_IN = Q_LORA_RANK + KV_LORA_RANK + QK_ROPE_DIM

V7X_VMEM_LIMIT_BYTES = 56 * 1024 * 1024

TOKEN_TILE = 512
FFN_TOKEN_TILE = 1024
FFN_SLAB = 256
DIL_Q_BLOCK = 128
DIL_MAX_BLOCK_ROWS = 4096
DIL_LSE_LANES = DIL_HEAD_DIM // DIL_HEADS_PER_GROUP
DIL_CHAINS = 32
DIL_SCALE = DIL_HEAD_DIM ** -0.5
DEINTERLEAVE_STRIDE = 4
MLA_CHUNKS_PER_TRIP = 4
MLA_TILES_PER_CHUNK = 2

_BF16 = jnp.bfloat16
_F32 = jnp.float32
_NT = (((1,), (1,)), ((), ()))
_TN = (((0,), (0,)), ((), ()))


def _params(*semantics):
    return pltpu.CompilerParams(dimension_semantics=semantics, vmem_limit_bytes=V7X_VMEM_LIMIT_BYTES)


def _resident(shape):
    return pl.BlockSpec(shape, lambda *_: (0,) * len(shape), pipeline_mode=pl.Buffered(1))


def _rms(xf, g):
    return xf * lax.rsqrt(jnp.mean(xf * xf, axis=-1, keepdims=True) + RMS_EPS) * g


def _dot(a, b):
    return jnp.dot(a, b, preferred_element_type=_F32)


def _ffn_kernel(x_ref, g_ref, wg_ref, wu_ref, wd_ref, fg_ref, o_ref, *, final_norm):
    x = x_ref[...]
    h = _rms(x, g_ref[...]).astype(_BF16)
    down = jnp.zeros_like(x)
    for c0 in range(0, D_FF, FFN_SLAB):
        gate = _dot(h, wg_ref[:, c0:c0 + FFN_SLAB])
        up = _dot(h, wu_ref[:, c0:c0 + FFN_SLAB])
        act = (gate * jax.nn.sigmoid(gate) * up).astype(_BF16)
        down = down + _dot(act, wd_ref[c0:c0 + FFN_SLAB, :])
    y = x + 0.5 * down
    if final_norm:
        y = _rms(y, fg_ref[...])
    o_ref[...] = y


def _ffn(x, norm_g, wg, wu, wd, final_g, *, final_norm):
    n_tok = x.shape[0]
    tm = FFN_TOKEN_TILE
    row = pl.BlockSpec((tm, D_MODEL), lambda i: (i, 0))
    return pl.pallas_call(
        functools.partial(_ffn_kernel, final_norm=final_norm),
        grid=(n_tok // tm,),
        in_specs=[row, _resident((1, D_MODEL)), _resident((D_MODEL, D_FF)), _resident((D_MODEL, D_FF)),
                  _resident((D_FF, D_MODEL)), _resident((1, D_MODEL))],
        out_specs=row,
        out_shape=jax.ShapeDtypeStruct((n_tok, D_MODEL), _F32),
        compiler_params=_params("parallel"),
        name="ffn",
    )(x, norm_g, wg, wu, wd, final_g)


def _proj_kernel(x_ref, g_ref, cos_ref, sin_ref, cos_t_ref, sin_t_ref, w_lat_ref, w_dil_ref, qn_ref,
                 kvn_ref, wq_ref, wk_ref, wvt_ref, *refs):
    qt_ref, k_ref, vt_ref = refs[:3]
    n_dil = 3 * len(DIL_PATTERNS)
    dil_refs = refs[3:3 + n_dil]
    stage_refs = refs[3 + n_dil:-1]
    restage_ref = refs[-1]
    tm = x_ref.shape[0]

    h = _rms(x_ref[...], g_ref[...]).astype(_BF16)
    cos = cos_ref[...]
    sin = sin_ref[...]

    lat = _dot(h, w_lat_ref[...])
    y_dil = [_dot(h, w_dil_ref[:, j * DIL_QKV:(j + 1) * DIL_QKV]) for j in range(3)]
    for j in range(3):
        for g, (_, dilation) in enumerate(DIL_PATTERNS):
            y = y_dil[j][:, g * DIL_OUT:(g + 1) * DIL_OUT]
            out = dil_refs[j * len(DIL_PATTERNS) + g]
            if dilation == 1:
                out[0, 0] = y.astype(_BF16)
            else:
                stage = stage_refs[g - 1]
                for c in range(DIL_HEADS_PER_GROUP):
                    cs = slice(c * DIL_HEAD_DIM, (c + 1) * DIL_HEAD_DIM)
                    stage[c] = y[:, cs]
                    if dilation <= DEINTERLEAVE_STRIDE:
                        for r in range(dilation):
                            out[0, r, :, cs] = stage[c, pl.ds(r, tm // dilation, stride=dilation), :].astype(_BF16)
                    else:
                        quarter, inner = tm // DEINTERLEAVE_STRIDE, dilation // DEINTERLEAVE_STRIDE
                        for r4 in range(DEINTERLEAVE_STRIDE):
                            restage_ref[c, r4 * quarter:(r4 + 1) * quarter, :] = \
                                stage[c, pl.ds(r4, quarter, stride=DEINTERLEAVE_STRIDE), :]
                        for r in range(dilation):
                            r4, r_in = r % DEINTERLEAVE_STRIDE, r // DEINTERLEAVE_STRIDE
                            out[0, r, :, cs] = restage_ref[
                                c, pl.ds(r4 * quarter + r_in, tm // dilation, stride=inner), :].astype(_BF16)

    c_q = _rms(lat[:, :Q_LORA_RANK], qn_ref[...]).astype(_BF16)
    c_kv = _rms(lat[:, Q_LORA_RANK:Q_LORA_RANK + KV_LORA_RANK], kvn_ref[...]).astype(_BF16)
    o = Q_LORA_RANK + KV_LORA_RANK
    k_pe = lat[:, o:o + MLA_HEAD_PAD] * cos + lat[:, o + MLA_HEAD_PAD:o + 2 * MLA_HEAD_PAD] * sin

    q_t = lax.dot_general(wq_ref[...], c_q, _NT, preferred_element_type=_F32) * MLA_Q_SCALE
    cos_c, sin_c = cos_t_ref[...], sin_t_ref[...]
    half = QK_ROPE_DIM // 2
    k_nope = _dot(c_kv, wk_ref[...])
    v_t = lax.dot_general(wvt_ref[...], c_kv, _NT, preferred_element_type=_F32).astype(_BF16)
    ones = jnp.ones((MLA_VT_ROWS - V_HEAD_DIM, tm), _BF16)
    for hd in range(MLA_HEADS):
        hs = slice(hd * MLA_HEAD_PAD, (hd + 1) * MLA_HEAD_PAD)
        r0 = hd * MLA_HEAD_PAD + QK_NOPE_DIM
        x1, x2 = q_t[r0:r0 + half], q_t[r0 + half:r0 + 2 * half]
        qt_ref[0, hs, :] = q_t[hs].astype(_BF16)
        qt_ref[0, r0:r0 + half, :] = (x1 * cos_c - x2 * sin_c).astype(_BF16)
        qt_ref[0, r0 + half:r0 + 2 * half, :] = (x2 * cos_c + x1 * sin_c).astype(_BF16)
        k_ref[:, hs] = (k_nope[:, hs] + k_pe).astype(_BF16)
        r0 = hd * MLA_VT_ROWS
        vt_ref[0, r0:r0 + V_HEAD_DIM, :] = v_t[hd * V_HEAD_DIM:(hd + 1) * V_HEAD_DIM]
        vt_ref[0, r0 + V_HEAD_DIM:r0 + MLA_VT_ROWS, :] = ones


def _residue_spec(tm, tiles_per_seq, dilation, width):
    return pl.BlockSpec((1, dilation, tm // dilation, width),
                        lambda i: (i // tiles_per_seq, 0, i % tiles_per_seq, 0))


def _proj(x, n_seq, seq, lw, rope):
    n_tok = x.shape[0]
    tm = TOKEN_TILE
    tiles_per_seq = seq // tm
    row = lambda w: pl.BlockSpec((tm, w), lambda i: (i, 0))
    cos_r, sin_r, cos_c, sin_c = rope
    rope_rows = pl.BlockSpec((tm, MLA_HEAD_PAD), lambda i: (i % tiles_per_seq, 0))
    rope_cols = pl.BlockSpec((QK_ROPE_DIM // 2, tm), lambda i: (0, i % tiles_per_seq))
    n_lat = lw["w_lat"].shape[1]
    mla_w = MLA_HEADS * MLA_HEAD_PAD
    vt_rows = MLA_HEADS * MLA_VT_ROWS
    dil_shapes = [jax.ShapeDtypeStruct((n_seq, d, seq // d, DIL_OUT), _BF16) for _ in range(3) for _, d in DIL_PATTERNS]
    dil_specs = [_residue_spec(tm, tiles_per_seq, d, DIL_OUT) for _ in range(3) for _, d in DIL_PATTERNS]
    out_shape = [jax.ShapeDtypeStruct((n_tok // tm, mla_w, tm), _BF16), jax.ShapeDtypeStruct((n_tok, mla_w), _BF16),
                 jax.ShapeDtypeStruct((n_tok // tm, vt_rows, tm), _BF16), *dil_shapes]
    out_specs = [pl.BlockSpec((1, mla_w, tm), lambda i: (i, 0, 0)), row(mla_w),
                 pl.BlockSpec((1, vt_rows, tm), lambda i: (i, 0, 0)), *dil_specs]
    outs = pl.pallas_call(
        _proj_kernel,
        grid=(n_tok // tm,),
        in_specs=[row(D_MODEL), _resident((1, D_MODEL)), rope_rows, rope_rows, rope_cols, rope_cols,
                  _resident((D_MODEL, n_lat)), _resident((D_MODEL, 3 * DIL_QKV)),
                  _resident((1, Q_LORA_RANK)), _resident((1, KV_LORA_RANK)),
                  _resident((mla_w, Q_LORA_RANK)),
                  _resident((KV_LORA_RANK, mla_w)), _resident((MLA_OUT, KV_LORA_RANK))],
        out_specs=out_specs,
        out_shape=out_shape,
        scratch_shapes=[pltpu.VMEM((DIL_HEADS_PER_GROUP, tm, DIL_HEAD_DIM), _F32) for _, d in DIL_PATTERNS if d > 1]
        + [pltpu.VMEM((DIL_HEADS_PER_GROUP, tm, DIL_HEAD_DIM), _F32)],
        compiler_params=_params("parallel"),
        name="proj",
    )(x, lw["mix_norm"], cos_r, sin_r, cos_c, sin_c, lw["w_lat"], lw["w_dil"], lw["q_a_norm"],
      lw["kv_a_norm"], lw["wq"], lw["wk"], lw["wvt"])
    q, k, vt = outs[:3]
    n_g = len(DIL_PATTERNS)
    dil_qkv = [tuple(outs[3 + j * n_g + g] for j in range(3)) for g in range(n_g)]
    return q, k, vt, dil_qkv


def _mla_kernel(qt_ref, k_ref, vt_ref, o_ref, s_ref, *, n_q, n_chunks, chunk, tq, per_trip):
    total = n_q * n_chunks

    def scores(g, slot):
        k0 = pl.multiple_of((g % n_chunks) * chunk, chunk)
        s_t = _dot(k_ref[pl.ds(k0, chunk), :], qt_ref[g // n_chunks])
        s_ref[slot] = s_t
        return jnp.max(s_t, axis=0, keepdims=True)

    def update(c, slot, m, m_chunk, acc):
        m_new = jnp.maximum(m, m_chunk)
        p = jnp.exp2(s_ref[slot] - m_new).astype(_BF16)
        v_t = jnp.concatenate([vt_ref[MLA_TILES_PER_CHUNK * c + t] for t in range(MLA_TILES_PER_CHUNK)], axis=1)
        return m_new, acc * jnp.exp2(m - m_new) + _dot(v_t, p)

    def trip(i, m_chunk):
        g0 = i * per_trip
        for j in range(per_trip):
            c = j % n_chunks
            if c == 0:
                m = jnp.full((1, tq), NEG_INF, _F32)
                acc = jnp.zeros((MLA_VT_ROWS, tq), _F32)
            m_next = scores(jnp.minimum(g0 + j + 1, total - 1), (j + 1) % per_trip)
            m, acc = update(c, j, m, m_chunk, acc)
            m_chunk = m_next
            if c == n_chunks - 1:
                o_ref[0, (g0 + j) // n_chunks] = (acc[:V_HEAD_DIM] / acc[V_HEAD_DIM:V_HEAD_DIM + 1]).astype(_BF16)
        return m_chunk

    lax.fori_loop(0, total // per_trip, trip, scores(0, 0))


def _mla(qt, k, vt, n_seq, seq):
    n_tok = k.shape[0]
    tq = qt.shape[2]
    chunk = MLA_TILES_PER_CHUNK * vt.shape[2]
    n_q = seq // tq
    n_chunks = seq // chunk
    per_trip = MLA_CHUNKS_PER_TRIP
    assert per_trip % n_chunks == 0 and (n_q * n_chunks) % per_trip == 0
    kern = functools.partial(_mla_kernel, n_q=n_q, n_chunks=n_chunks, chunk=chunk, tq=tq, per_trip=per_trip)
    return pl.pallas_call(
        kern,
        grid=(n_seq, MLA_HEADS),
        in_specs=[pl.BlockSpec((n_q, MLA_HEAD_PAD, tq), lambda b, h: (b, h, 0)),
                  pl.BlockSpec((seq, MLA_HEAD_PAD), lambda b, h: (b, h)),
                  pl.BlockSpec((seq // vt.shape[2], MLA_VT_ROWS, vt.shape[2]), lambda b, h: (b, h, 0))],
        out_specs=pl.BlockSpec((1, n_q, V_HEAD_DIM, tq), lambda b, h: (h, b, 0, 0)),
        out_shape=jax.ShapeDtypeStruct((MLA_HEADS, n_tok // tq, V_HEAD_DIM, tq), _BF16),
        scratch_shapes=[pltpu.VMEM((per_trip, chunk, tq), _F32)],
        compiler_params=_params("parallel", "parallel"),
        name="mla_attention",
    )(qt, k, vt)


def _dilated_geometry(seg, half):
    bq = min(DIL_Q_BLOCK, seg)
    kw = min(bq + 2 * half, seg)
    return bq, kw, seg // bq


def _dilated_bias(slopes, group, seg):
    window, dilation = DIL_PATTERNS[group]
    half = window // (2 * dilation)
    bq, kw, _ = _dilated_geometry(seg, half)
    rel = jnp.arange(kw)[None, :] - jnp.arange(bq)[:, None]
    hs = slice(group * DIL_HEADS_PER_GROUP, (group + 1) * DIL_HEADS_PER_GROUP)
    tables = []
    for off in (0, -half, bq - kw):
        dist = jnp.abs(rel + off)
        alibi = -slopes[hs, None, None] * (dilation * dist).astype(_F32)[None]
        tables.append(jnp.where((dist <= half)[None], alibi / DIL_SCALE, NEG_INF))
    return jnp.stack(tables)


def _dilated_kernel(bias_ref, q_ref, k_ref, v_ref, o_ref, lse_ref, *, half, seg, residues):
    bq, kw, n_blk = _dilated_geometry(seg, half)
    exp2_scale = DIL_SCALE * math.log2(math.e)
    heads = DIL_HEADS_PER_GROUP
    n_units = residues * n_blk
    units_per_trip = min(n_units, max(1, DIL_CHAINS // heads))
    lane_head = lax.broadcasted_iota(jnp.int32, (bq, DIL_HEAD_DIM), 1) // DIL_LSE_LANES

    def body(t, carry):
        units = []
        for j in range(units_per_trip):
            u = t * units_per_trip + j
            r = u // n_blk
            i = u % n_blk
            q0 = pl.multiple_of(i * bq, bq)
            ks = pl.multiple_of(jnp.clip(q0 - half, 0, seg - kw), half)
            variant = jnp.where(i == 0, 0, jnp.where(i == n_blk - 1, 2, 1))
            units.append((r, q0, ks, variant))
        chains = [(r, q0, ks, variant, hd, slice(hd * DIL_HEAD_DIM, (hd + 1) * DIL_HEAD_DIM))
                  for r, q0, ks, variant in units for hd in range(heads)]
        x = jnp.concatenate(
            [lax.dot_general(q_ref[0, r, pl.ds(q0, bq), hs], k_ref[0, r, pl.ds(ks, kw), hs], _NT,
                             preferred_element_type=_F32) + bias_ref[variant, hd]
             for r, q0, ks, variant, hd, hs in chains], axis=0)
        m = jnp.max(x, axis=-1, keepdims=True)
        p = jnp.exp2((x - m) * exp2_scale)
        l = jnp.sum(p, axis=-1, keepdims=True)
        p = p.astype(_BF16)
        lse = m * DIL_SCALE + jnp.log(l)
        for c, (r, q0, ks, variant, hd, hs) in enumerate(chains):
            rows = slice(c * bq, (c + 1) * bq)
            o = _dot(p[rows], v_ref[0, r, pl.ds(ks, kw), hs]) / l[rows]
            o_ref[0, r, pl.ds(q0, bq), hs] = o.astype(_BF16)
        for j, (r, q0, ks, variant) in enumerate(units):
            tile = lse[(j * heads) * bq:(j * heads + 1) * bq]
            for hd in range(1, heads):
                tile = jnp.where(lane_head == hd, lse[(j * heads + hd) * bq:(j * heads + hd + 1) * bq], tile)
            lse_ref[0, r, pl.ds(q0, bq), :] = tile
        return carry

    lax.fori_loop(0, n_units // units_per_trip, body, 0)


def _dilated(bias, dq, dk, dv, group):
    window, dilation = DIL_PATTERNS[group]
    n_seq, _, seg, _ = dq.shape
    half = window // (2 * dilation)
    residues = max(1, min(dilation, DIL_MAX_BLOCK_ROWS // seg))
    spec = lambda w: pl.BlockSpec((1, residues, seg, w), lambda b, r: (b, r, 0, 0))
    kern = functools.partial(_dilated_kernel, half=half, seg=seg, residues=residues)
    return pl.pallas_call(
        kern,
        grid=(n_seq, dilation // residues),
        in_specs=[_resident(bias.shape), spec(DIL_OUT), spec(DIL_OUT), spec(DIL_OUT)],
        out_specs=(spec(DIL_OUT), spec(DIL_HEAD_DIM)),
        out_shape=(jax.ShapeDtypeStruct(dq.shape, _BF16),
                   jax.ShapeDtypeStruct(dq.shape[:3] + (DIL_HEAD_DIM,), _F32)),
        compiler_params=_params("parallel", "parallel"),
        name=f"dilated_g{group}",
    )(bias, dq, dk, dv)


def _interleave_rows(stage, restage, residue_rows, tm, dilation):
    if dilation <= DEINTERLEAVE_STRIDE:
        for r in range(dilation):
            stage[pl.ds(r, tm // dilation, stride=dilation), :] = residue_rows(r)
        return
    quarter, inner = tm // DEINTERLEAVE_STRIDE, dilation // DEINTERLEAVE_STRIDE
    for r in range(dilation):
        r4, r_in = r % DEINTERLEAVE_STRIDE, r // DEINTERLEAVE_STRIDE
        restage[pl.ds(r4 * quarter + r_in, tm // dilation, stride=inner), :] = residue_rows(r)
    for r4 in range(DEINTERLEAVE_STRIDE):
        stage[pl.ds(r4, quarter, stride=DEINTERLEAVE_STRIDE), :] = restage[r4 * quarter:(r4 + 1) * quarter, :]


def _merge_kernel(x_ref, omla_t_ref, o0_ref, o1_ref, o2_ref, l0_ref, l1_ref, l2_ref, g_ref, w_gate_ref, b_gate_ref,
                  wm_ref, wd_ref, wo_ref, out_ref, *stage_refs):
    tm = x_ref.shape[0]
    x = x_ref[...]
    gates = jax.nn.sigmoid(_dot(_rms(x, g_ref[...]).astype(_BF16), w_gate_ref[...]) + b_gate_ref[...])
    lse_stages, o_stages, restage_refs = stage_refs[:2], stage_refs[2:4], stage_refs[4:]

    def lse_token_order(ref, stage):
        dilation = ref.shape[1]
        if dilation == 1:
            return ref[0, 0]
        _interleave_rows(stage, restage_refs[0], lambda r: ref[0, r], tm, dilation)
        return stage[...]

    def o_token_order(ref, stage, hd):
        dilation = ref.shape[1]
        hs = slice(hd * DIL_HEAD_DIM, (hd + 1) * DIL_HEAD_DIM)
        if dilation == 1:
            return ref[0, 0, :, hs].astype(_F32)
        _interleave_rows(stage.at[hd], restage_refs[1], lambda r: ref[0, r, :, hs].astype(_F32), tm, dilation)
        return stage[hd]

    l0 = lse_token_order(l0_ref, None)
    l1 = lse_token_order(l1_ref, lse_stages[0])
    l2 = lse_token_order(l2_ref, lse_stages[1])
    m = jnp.maximum(jnp.maximum(l0, l1), l2)
    e0, e1, e2 = jnp.exp(l0 - m), jnp.exp(l1 - m), jnp.exp(l2 - m)
    den = e0 + e1 + e2
    w0, w1, w2 = e0 / den, e1 / den, e2 / den
    o_heads = []
    for hd in range(DIL_HEADS_PER_GROUP):
        lane = slice(hd * DIL_LSE_LANES, hd * DIL_LSE_LANES + 1)
        o_heads.append(w0[:, lane] * o_token_order(o0_ref, None, hd)
                       + w1[:, lane] * o_token_order(o1_ref, o_stages[0], hd)
                       + w2[:, lane] * o_token_order(o2_ref, o_stages[1], hd))
    o_dil = jnp.concatenate(o_heads, axis=1)
    omla_t = omla_t_ref[:, 0].reshape(MLA_OUT, tm)
    y_mla = lax.dot_general(omla_t, wm_ref[...], _TN, preferred_element_type=_F32)
    y_dil = _dot(o_dil.astype(_BF16), wd_ref[...])
    merged = gates[:, :D_MODEL] * y_mla + gates[:, D_MODEL:] * y_dil
    out_ref[...] = x + _dot(merged.astype(_BF16), wo_ref[...])


def _merge(x, seq, omla_t, dil_o, dil_lse, lw):
    n_tok = x.shape[0]
    tm = TOKEN_TILE
    tiles_per_seq = seq // tm
    row = lambda w: pl.BlockSpec((tm, w), lambda i: (i, 0))
    o_specs = [_residue_spec(tm, tiles_per_seq, d, DIL_OUT) for _, d in DIL_PATTERNS]
    lse_specs = [_residue_spec(tm, tiles_per_seq, d, DIL_HEAD_DIM) for _, d in DIL_PATTERNS]
    n_strided = sum(d > 1 for _, d in DIL_PATTERNS)
    return pl.pallas_call(
        _merge_kernel,
        grid=(n_tok // tm,),
        in_specs=[row(D_MODEL), pl.BlockSpec((MLA_HEADS, 1, V_HEAD_DIM, tm), lambda i: (0, i, 0, 0)),
                  *o_specs, *lse_specs,
                  _resident((1, D_MODEL)), _resident((D_MODEL, 2 * D_MODEL)), _resident((1, 2 * D_MODEL)),
                  _resident((MLA_OUT, D_MODEL)), _resident((DIL_OUT, D_MODEL)), _resident((D_MODEL, D_MODEL))],
        out_specs=row(D_MODEL),
        out_shape=jax.ShapeDtypeStruct((n_tok, D_MODEL), _F32),
        scratch_shapes=[pltpu.VMEM((tm, DIL_HEAD_DIM), _F32)] * n_strided
        + [pltpu.VMEM((DIL_HEADS_PER_GROUP, tm, DIL_HEAD_DIM), _F32)] * n_strided
        + [pltpu.VMEM((tm, DIL_HEAD_DIM), _F32)] * 2,
        compiler_params=_params("parallel"),
        name="merge",
    )(x, omla_t, *dil_o, *dil_lse, lw["mix_norm"], lw["w_gate"], lw["b_gate"], lw["w_branch_mla"],
      lw["w_branch_dil"], lw["w_out"])


def _prep_layer(l, ffn1_norm, ffn1_w_gate, ffn1_w_up, ffn1_w_down, mix_norm, w_in, b_gate, q_a_norm, w_q_up,
                kv_a_norm, w_kv_up, w_branch_mla, w_branch_dil, w_out, ffn2_norm, ffn2_w_gate, ffn2_w_up,
                ffn2_w_down):
    half = QK_ROPE_DIM // 2
    w = w_in[l]
    zeros = lambda r, c: jnp.zeros((r, c), _F32)

    k_pe = w[:, Q_LORA_RANK + KV_LORA_RANK:MLA_IN]
    pad_lo, pad_hi = QK_NOPE_DIM, MLA_HEAD_PAD - QK_NOPE_DIM - QK_ROPE_DIM
    k_pe_cols = jnp.concatenate([zeros(D_MODEL, pad_lo), k_pe, zeros(D_MODEL, pad_hi)], axis=1)
    k_rot_cols = jnp.concatenate([zeros(D_MODEL, pad_lo), -k_pe[:, half:], k_pe[:, :half], zeros(D_MODEL, pad_hi)],
                                 axis=1)
    w_lat = jnp.concatenate([w[:, :Q_LORA_RANK + KV_LORA_RANK], k_pe_cols, k_rot_cols], axis=1)

    wq = w_q_up[l].reshape(Q_LORA_RANK, MLA_HEADS, QK_NOPE_DIM + QK_ROPE_DIM)
    zq = jnp.zeros((Q_LORA_RANK, MLA_HEADS, pad_hi), _F32)
    wq_pad = jnp.concatenate([wq, zq], axis=2).reshape(Q_LORA_RANK, -1)
    wkv = w_kv_up[l].reshape(KV_LORA_RANK, MLA_HEADS, QK_NOPE_DIM + V_HEAD_DIM)
    wk = jnp.concatenate([wkv[:, :, :QK_NOPE_DIM],
                          jnp.zeros((KV_LORA_RANK, MLA_HEADS, MLA_HEAD_PAD - QK_NOPE_DIM), _F32)],
                         axis=2).reshape(KV_LORA_RANK, -1)
    wvt = wkv[:, :, QK_NOPE_DIM:].reshape(KV_LORA_RANK, MLA_OUT).T

    bf = lambda t: t.astype(_BF16)
    vec = lambda t: t.reshape(1, -1)
    return dict(
        ffn1=(vec(ffn1_norm[l]), bf(ffn1_w_gate[l]), bf(ffn1_w_up[l]), bf(ffn1_w_down[l])),
        ffn2=(vec(ffn2_norm[l]), bf(ffn2_w_gate[l]), bf(ffn2_w_up[l]), bf(ffn2_w_down[l])),
        mix_norm=vec(mix_norm[l]), w_lat=bf(w_lat), w_dil=bf(w[:, MLA_IN:MLA_IN + 3 * DIL_QKV]),
        w_gate=bf(w[:, MLA_IN + 3 * DIL_QKV:]), b_gate=vec(b_gate[l]),
        q_a_norm=vec(q_a_norm[l]), kv_a_norm=vec(kv_a_norm[l]),
        wq=bf(wq_pad.T), wk=bf(wk), wvt=bf(wvt),
        w_branch_mla=bf(w_branch_mla[l]), w_branch_dil=bf(w_branch_dil[l]), w_out=bf(w_out[l]),
    )


def _rope_tables(seq):
    pos = jnp.arange(seq, dtype=_F32)
    inv_freq = 1.0 / (ROPE_THETA ** (jnp.arange(0, QK_ROPE_DIM, 2, dtype=_F32) / QK_ROPE_DIM))
    ang = pos[:, None] * inv_freq[None, :]
    cos, sin = jnp.cos(ang), jnp.sin(ang)
    pad_hi = MLA_HEAD_PAD - QK_NOPE_DIM - QK_ROPE_DIM
    cos_t = jnp.concatenate([jnp.ones((seq, QK_NOPE_DIM), _F32), cos, cos, jnp.ones((seq, pad_hi), _F32)], axis=1)
    sin_t = jnp.concatenate([jnp.zeros((seq, QK_NOPE_DIM), _F32), sin, sin, jnp.zeros((seq, pad_hi), _F32)], axis=1)
    return cos_t, sin_t, cos.T, sin.T


def _alibi_slopes(n):
    return 2.0 ** (-8.0 * jnp.arange(1, n + 1, dtype=_F32) / n)


def _trunk(x, layers, final_g):
    n_seq, seq, _ = x.shape
    x = x.reshape(n_seq * seq, D_MODEL)
    rope = _rope_tables(seq)
    slopes = _alibi_slopes(DIL_HEADS)
    biases = [_dilated_bias(slopes, g, seq // d) for g, (_, d) in enumerate(DIL_PATTERNS)]
    for l, lw in enumerate(layers):
        x = _ffn(x, *lw["ffn1"], final_g, final_norm=False)
        qt, k, vt, dil_qkv = _proj(x, n_seq, seq, lw, rope)
        omla_t = _mla(qt, k, vt, n_seq, seq)
        dil = [_dilated(biases[g], *dil_qkv[g], g) for g in range(len(DIL_PATTERNS))]
        x = _merge(x, seq, omla_t, [o for o, _ in dil], [lse for _, lse in dil], lw)
        x = _ffn(x, *lw["ffn2"], final_g, final_norm=(l == len(layers) - 1))
    return x.reshape(n_seq, seq, D_MODEL)


def kernel(x_prompt, x_sample, ffn1_norm, ffn1_w_gate, ffn1_w_up, ffn1_w_down, mix_norm, w_in, b_gate, q_a_norm,
           w_q_up, kv_a_norm, w_kv_up, w_branch_mla, w_branch_dil, w_out, ffn2_norm, ffn2_w_gate, ffn2_w_up,
           ffn2_w_down, final_norm):
    stacked = (ffn1_norm, ffn1_w_gate, ffn1_w_up, ffn1_w_down, mix_norm, w_in, b_gate, q_a_norm, w_q_up, kv_a_norm,
               w_kv_up, w_branch_mla, w_branch_dil, w_out, ffn2_norm, ffn2_w_gate, ffn2_w_up, ffn2_w_down)
    layers = [_prep_layer(l, *stacked) for l in range(ffn1_norm.shape[0])]
    final_g = final_norm.reshape(1, -1)
    return (_trunk(x_prompt, layers, final_g), _trunk(x_sample, layers, final_g))
```
